```python
import math
import jax, jax.numpy as jnp
from jax import lax
import numpy as np

D_MODEL = 1024
BATCH = 16
SEQ = 2048
DEPTH = 2

MIX_WIDTH = D_MODEL // 2
HY_WIDTH = MIX_WIDTH
HY_ORDER = 2
HY_POS_BANDS = 8
HY_EMB = 1 + 2 * HY_POS_BANDS
HY_FILTER_HIDDEN = 64
HY_DECAY_TARGET = 1e-2
HY_FAST_DECAY = 0.3
HY_SLOW_DECAY = 1.5
HY_MOD_SHIFT = 0.05
HEAD_DIM = 64
N_Q_HEADS = MIX_WIDTH // HEAD_DIM
N_KV_HEADS = N_Q_HEADS // 4
KV_WIDTH = N_KV_HEADS * HEAD_DIM
WINDOW = 128
BLOCK = 128
ROPE_THETA = 10000.0
GM_WIDTH = MIX_WIDTH
GM_GROUPS = 4
GM_CHUNK = 128
GM_GROUP_CH = GM_WIDTH // GM_GROUPS
N_BRANCH = 3
N_EXPERTS = 16
EC_CAPACITY = 2
D_EXPERT = 2 * D_MODEL
EPS = 1e-6
NEG_INF = -1e30

IN_SIZES = (3 * HY_WIDTH, N_Q_HEADS * HEAD_DIM, KV_WIDTH, KV_WIDTH, GM_WIDTH, GM_WIDTH, N_BRANCH * D_MODEL)
IN_COLS = 3 * HY_WIDTH + N_Q_HEADS * HEAD_DIM + 2 * KV_WIDTH + 2 * GM_WIDTH + N_BRANCH * D_MODEL
IN_SPLITS = (
    3 * HY_WIDTH,
    3 * HY_WIDTH + N_Q_HEADS * HEAD_DIM,
    3 * HY_WIDTH + N_Q_HEADS * HEAD_DIM + KV_WIDTH,
    3 * HY_WIDTH + N_Q_HEADS * HEAD_DIM + 2 * KV_WIDTH,
    3 * HY_WIDTH + N_Q_HEADS * HEAD_DIM + 2 * KV_WIDTH + GM_WIDTH,
    3 * HY_WIDTH + N_Q_HEADS * HEAD_DIM + 2 * KV_WIDTH + 2 * GM_WIDTH,
)

kernel_name = "hybrid_hyena_swa_gmlp_ec_block"


def _rmsnorm(x, g):
    xf = x.astype(jnp.float32)
    y = xf * lax.rsqrt(jnp.mean(xf * xf, axis=-1, keepdims=True) + EPS) * g.astype(jnp.float32)
    return y.astype(x.dtype)


def _layernorm(x, g, b):
    xf = x.astype(jnp.float32)
    mu = jnp.mean(xf, axis=-1, keepdims=True)
    var = jnp.mean(jnp.square(xf - mu), axis=-1, keepdims=True)
    y = (xf - mu) * lax.rsqrt(var + EPS) * g.astype(jnp.float32) + b.astype(jnp.float32)
    return y.astype(x.dtype)


def _short_conv(z, w, b):
    zp = jnp.pad(z, ((0, 0), (1, 1), (0, 0)))
    return zp[:, :-2] * w[0] + zp[:, 1:-1] * w[1] + zp[:, 2:] * w[2] + b


def _hyena_filters(L, f1_w, f1_b, f1_freq, f2_w, f2_b, f2_freq, f3_w):
    f32 = jnp.float32
    pos = jnp.arange(L, dtype=f32)
    t = jnp.linspace(0.0, 1.0, L, dtype=f32)
    w = 2.0 * math.pi * pos / L
    bands = jnp.linspace(1e-4, HY_POS_BANDS - 1, HY_POS_BANDS, dtype=f32)
    ang = w[:, None] * bands[None, :]
    feats = jnp.concatenate([t[:, None], jnp.cos(ang), -jnp.sin(ang)], axis=-1)
    a = jnp.sin(f1_freq.astype(f32) * (feats @ f1_w.astype(f32) + f1_b.astype(f32)))
    a = jnp.sin(f2_freq.astype(f32) * (a @ f2_w.astype(f32) + f2_b.astype(f32)))
    h = (a @ f3_w.astype(f32)).reshape(L, 2, HY_ORDER, HY_WIDTH)
    deltas = jnp.abs(jnp.linspace(math.log(HY_DECAY_TARGET) / HY_FAST_DECAY,
                                  math.log(HY_DECAY_TARGET) / HY_SLOW_DECAY, HY_WIDTH, dtype=f32))
    window = jnp.exp(-t[:, None] * deltas[None, :]) + HY_MOD_SHIFT
    h = h * window[:, None, None, :]
    k = jnp.concatenate([h[:, 0], jnp.zeros((1, HY_ORDER, HY_WIDTH), f32), h[:0:-1, 1]], axis=0)
    k = k / jnp.sum(jnp.abs(k), axis=0, keepdims=True)
    return jnp.fft.rfft(k, axis=0)


def _fft_conv(u, kf, bias):
    L = u.shape[1]
    uf = u.astype(jnp.float32)
    y = jnp.fft.irfft(jnp.fft.rfft(uf, n=2 * L, axis=1) * kf[None], n=2 * L, axis=1)[:, :L]
    return (y + uf * bias.astype(jnp.float32)).astype(u.dtype)


def _hyena(z, conv_w, conv_b, f1_w, f1_b, f1_freq, f2_w, f2_b, f2_freq, f3_w, bias):
    z = _short_conv(z, conv_w, conv_b)
    v, x1, x2 = jnp.split(z, 3, axis=-1)
    kf = _hyena_filters(z.shape[1], f1_w, f1_b, f1_freq, f2_w, f2_b, f2_freq, f3_w)
    y = x1 * _fft_conv(v, kf[:, 0], bias[0])
    y = x2 * _fft_conv(y, kf[:, 1], bias[1])
    return y


def _rotary(x, cos, sin):
    x1, x2 = jnp.split(x, 2, axis=-1)
    c = cos[None, :, None, :]
    s = sin[None, :, None, :]
    return jnp.concatenate([x1 * c - x2 * s, x2 * c + x1 * s], axis=-1)


def _window_attention(q, k, v, sink):
    B, S, H, hd = q.shape
    nb = S // BLOCK
    G = H // N_KV_HEADS
    qb = q.reshape(B, nb, BLOCK, N_KV_HEADS, G, hd)

    def band(t):
        tp = jnp.pad(t, ((0, 0), (BLOCK, BLOCK), (0, 0), (0, 0))).reshape(B, nb + 2, BLOCK, N_KV_HEADS, hd)
        return jnp.concatenate([tp[:, :-2], tp[:, 1:-1], tp[:, 2:]], axis=2)

    kb = band(k)
    vb = band(v)
    s = jnp.einsum('bnqkgd,bnskd->bnkgqs', qb, kb).astype(jnp.float32) * (hd ** -0.5)
    blk = jnp.arange(nb)[:, None, None] * BLOCK
    qpos = blk + jnp.arange(BLOCK)[None, :, None]
    kpos = blk - BLOCK + jnp.arange(3 * BLOCK)[None, None, :]
    valid = (jnp.abs(kpos - qpos) <= WINDOW) & (kpos >= 0) & (kpos < S)
    s = jnp.where(valid[None, :, None, None], s, NEG_INF)
    sink_l = jnp.broadcast_to(sink.astype(jnp.float32).reshape(N_KV_HEADS, G)[None, None, :, :, None, None],
                              s.shape[:-1] + (1,))
    p = jax.nn.softmax(jnp.concatenate([s, sink_l], axis=-1), axis=-1)[..., :-1]
    o = jnp.einsum('bnkgqs,bnskd->bnqkgd', p.astype(v.dtype), vb)
    return o.reshape(B, S, H * hd)


def _chunk_sgu(zu, zv, ln_g, ln_b, ws, b):
    B, S, _ = zu.shape
    nc = S // GM_CHUNK
    u = jax.nn.gelu(zu, approximate=False)
    v = _layernorm(jax.nn.gelu(zv, approximate=False), ln_g, ln_b)
    vc = v.reshape(B, nc, GM_CHUNK, GM_GROUPS, GM_GROUP_CH)
    sv = jnp.einsum('gpq,bnqgc->bnpgc', ws, vc) + b.T[None, None, :, :, None]
    return u * sv.reshape(B, S, GM_WIDTH)


def _expert_choice_ffn(h, w_router, w_gate, w_up, w_down):
    B, S, D = h.shape
    cap = EC_CAPACITY * S // N_EXPERTS
    aff = jax.nn.softmax((h @ w_router).astype(jnp.float32), axis=-1)
    gate, idx = lax.top_k(jnp.swapaxes(aff, 1, 2), cap)
    xg = jax.vmap(lambda hb, ib: hb[ib])(h, idx)
    a = jnp.einsum('becd,edf->becf', xg, w_gate)
    u = jnp.einsum('becd,edf->becf', xg, w_up)
    y = jnp.einsum('becf,efd->becd', jax.nn.silu(a) * u, w_down) * gate[..., None].astype(h.dtype)
    seg = (jnp.arange(B)[:, None, None] * S + idx).reshape(-1)
    out = jax.ops.segment_sum(y.reshape(-1, D), seg, num_segments=B * S)
    return out.reshape(B, S, D)


def _normal(k, shape, scale):
    return jax.random.normal(k, shape, jnp.float32) * scale


def setup_inputs(seed: int = 0) -> dict:
    key = jax.random.key(seed)
    ks = jax.random.split(key, 32)
    L = DEPTH
    return {
        "x": _normal(ks[0], (BATCH, SEQ, D_MODEL), 1.0),
        "c": _normal(ks[1], (BATCH, D_MODEL), 1.0),
        "w_mod": _normal(ks[2], (L, D_MODEL, 6 * D_MODEL), D_MODEL ** -0.5),
        "b_mod": _normal(ks[3], (L, 6 * D_MODEL), 0.02),
        "norm1_g": 1.0 + _normal(ks[4], (L, D_MODEL), 0.02),
        "norm2_g": 1.0 + _normal(ks[5], (L, D_MODEL), 0.02),
        "w_in": _normal(ks[6], (L, D_MODEL, IN_COLS), D_MODEL ** -0.5),
        "hy_conv_w": _normal(ks[7], (L, 3, 3 * HY_WIDTH), 3 ** -0.5),
        "hy_conv_b": _normal(ks[8], (L, 3 * HY_WIDTH), 0.02),
        "hy_f1_w": _normal(ks[9], (L, HY_EMB, HY_FILTER_HIDDEN), HY_EMB ** -0.5),
        "hy_f1_b": _normal(ks[10], (L, HY_FILTER_HIDDEN), 0.1),
        "hy_f1_freq": 1.0 + _normal(ks[11], (L, HY_FILTER_HIDDEN), 0.02),
        "hy_f2_w": _normal(ks[12], (L, HY_FILTER_HIDDEN, HY_FILTER_HIDDEN), HY_FILTER_HIDDEN ** -0.5),
        "hy_f2_b": _normal(ks[13], (L, HY_FILTER_HIDDEN), 0.1),
        "hy_f2_freq": 1.0 + _normal(ks[14], (L, HY_FILTER_HIDDEN), 0.02),
        "hy_f3_w": _normal(ks[15], (L, HY_FILTER_HIDDEN, 2 * HY_ORDER * HY_WIDTH), HY_FILTER_HIDDEN ** -0.5),
        "hy_bias": _normal(ks[16], (L, HY_ORDER, HY_WIDTH), 0.5),
        "q_norm_g": 1.0 + _normal(ks[17], (L, HEAD_DIM), 0.02),
        "k_norm_g": 1.0 + _normal(ks[18], (L, HEAD_DIM), 0.02),
        "attn_sink": _normal(ks[19], (L, N_Q_HEADS), 0.5),
        "gm_ln_g": 1.0 + _normal(ks[20], (L, GM_WIDTH), 0.02),
        "gm_ln_b": _normal(ks[21], (L, GM_WIDTH), 0.02),
        "gm_ws": _normal(ks[22], (L, GM_GROUPS, GM_CHUNK, GM_CHUNK), GM_CHUNK ** -0.5),
        "gm_b": 1.0 + _normal(ks[23], (L, GM_GROUPS, GM_CHUNK), 0.02),
        "w_branch": _normal(ks[24], (L, N_BRANCH, MIX_WIDTH, D_MODEL), MIX_WIDTH ** -0.5),
        "w_out": _normal(ks[25], (L, D_MODEL, D_MODEL), D_MODEL ** -0.5),
        "w_router": _normal(ks[26], (L, D_MODEL, N_EXPERTS), D_MODEL ** -0.5),
        "w_e_gate": _normal(ks[27], (L, N_EXPERTS, D_MODEL, D_EXPERT), D_MODEL ** -0.5),
        "w_e_up": _normal(ks[28], (L, N_EXPERTS, D_MODEL, D_EXPERT), D_MODEL ** -0.5),
        "w_e_down": _normal(ks[29], (L, N_EXPERTS, D_EXPERT, D_MODEL), D_EXPERT ** -0.5),
    }


def reference(x, c, w_mod, b_mod, norm1_g, norm2_g, w_in, hy_conv_w, hy_conv_b, hy_f1_w, hy_f1_b,
              hy_f1_freq, hy_f2_w, hy_f2_b, hy_f2_freq, hy_f3_w, hy_bias, q_norm_g, k_norm_g, attn_sink,
              gm_ln_g, gm_ln_b, gm_ws, gm_b, w_branch, w_out, w_router, w_e_gate, w_e_up, w_e_down):
    B, S, _ = x.shape
    pos = jnp.arange(S, dtype=jnp.float32)
    inv = ROPE_THETA ** (-jnp.arange(0, HEAD_DIM, 2, dtype=jnp.float32) / HEAD_DIM)
    ang = pos[:, None] * inv[None, :]
    cos = jnp.cos(ang).astype(x.dtype)
    sin = jnp.sin(ang).astype(x.dtype)
    cond = jax.nn.silu(c)
    for l in range(DEPTH):
        mod = cond @ w_mod[l] + b_mod[l]
        sh1, sc1, gt1, sh2, sc2, gt2 = [m[:, None, :] for m in jnp.split(mod, 6, axis=-1)]
        h = _rmsnorm(x, norm1_g[l]) * (1 + sc1) + sh1
        z = h @ w_in[l]
        z_hy, z_q, z_k, z_v, z_gu, z_gv, z_gate = jnp.split(z, IN_SPLITS, axis=-1)
        y_hy = _hyena(z_hy, hy_conv_w[l], hy_conv_b[l], hy_f1_w[l], hy_f1_b[l], hy_f1_freq[l],
                      hy_f2_w[l], hy_f2_b[l], hy_f2_freq[l], hy_f3_w[l], hy_bias[l])
        q = _rotary(_rmsnorm(z_q.reshape(B, S, N_Q_HEADS, HEAD_DIM), q_norm_g[l]), cos, sin)
        k = _rotary(_rmsnorm(z_k.reshape(B, S, N_KV_HEADS, HEAD_DIM), k_norm_g[l]), cos, sin)
        y_at = _window_attention(q, k, z_v.reshape(B, S, N_KV_HEADS, HEAD_DIM), attn_sink[l])
        y_gm = _chunk_sgu(z_gu, z_gv, gm_ln_g[l], gm_ln_b[l], gm_ws[l], gm_b[l])
        ys = jnp.stack([y_hy, y_at, y_gm], axis=2)
        br = jnp.einsum('bsiw,iwd->bsid', ys, w_branch[l])
        gates = jax.nn.sigmoid(z_gate.reshape(B, S, N_BRANCH, D_MODEL))
        mix = jnp.sum(gates * br, axis=2) @ w_out[l]
        x = x + gt1 * mix
        h = _rmsnorm(x, norm2_g[l]) * (1 + sc2) + sh2
        x = x + gt2 * _expert_choice_ffn(h, w_router[l], w_e_gate[l], w_e_up[l], w_e_down[l])
    return x
```

```python
import functools
import math

import numpy as np
import jax
import jax.numpy as jnp
from jax import lax
from jax.experimental import pallas as pl
from jax.experimental.pallas import tpu as pltpu

F32 = jnp.float32
BF16 = jnp.bfloat16
HIGHEST = lax.Precision.HIGHEST

HEAD_DIM = 64
N_Q_HEADS = 8
N_KV_HEADS = 2
WINDOW = 128
ATT_BLOCK = 128
ROPE_THETA = 10000.0
HY_POS_BANDS = 8
HY_DECAY_TARGET = 1e-2
HY_FAST_DECAY = 0.3
HY_SLOW_DECAY = 1.5
HY_MOD_SHIFT = 0.05
GM_GROUPS = 4
GM_CHUNK = 128
N_EXPERTS = 16
EC_CAPACITY = 2
EPS = 1e-6
NEG_INF = -1e30

V7X_VMEM_LIMIT_BYTES = 56 * 1024 * 1024
LANES = 128


def _cparams(*sem):
    return pltpu.CompilerParams(dimension_semantics=sem, vmem_limit_bytes=V7X_VMEM_LIMIT_BYTES)


def _resident(shape, index_map):
    return pl.BlockSpec(shape, index_map, pipeline_mode=pl.Buffered(1))


def _dot(a, b, precision=None):
    return jnp.dot(a, b, preferred_element_type=F32, precision=precision)


def _dot_nt(a, b):
    return lax.dot_general(a, b, (((1,), (1,)), ((), ())), preferred_element_type=F32)


def _split_bf16(x):
    hi = x.astype(BF16)
    return hi, (x - hi.astype(F32)).astype(BF16)


def _mod_kernel(c_ref, w_ref, b_ref, o_ref):
    c = c_ref[...]
    cond = c * jax.nn.sigmoid(c)
    o_ref[...] = _dot(cond, w_ref[...], HIGHEST) + b_ref[...]


def _modulation(c, w_mod, b_mod):
    depth, d, _ = w_mod.shape
    b = c.shape[0]
    out = pl.pallas_call(
        _mod_kernel,
        grid=(depth, 6),
        in_specs=[
            pl.BlockSpec((b, d), lambda l, j: (0, 0)),
            pl.BlockSpec((None, d, d), lambda l, j: (l, 0, j)),
            pl.BlockSpec((None, None, 1, d), lambda l, j: (l, j, 0, 0)),
        ],
        out_specs=pl.BlockSpec((None, None, b, d), lambda l, j: (l, j, 0, 0)),
        out_shape=jax.ShapeDtypeStruct((depth, 6, b, d), F32),
        compiler_params=_cparams("arbitrary", "arbitrary"),
        name="mod",
    )(c, w_mod, b_mod.reshape(depth, 6, 1, d))
    return out.reshape(depth, 6, b, 1, d)


_W_HY, _W_Q, _W_KV, _W_GM, _W_GATE = 1536, 512, 128, 512, 3072
_C_Q = _W_HY
_C_K = _C_Q + _W_Q
_C_V = _C_K + _W_KV
_C_GU = _C_V + _W_KV
_C_GV = _C_GU + _W_GM
_C_GATE = _C_GV + _W_GM
IN_COLS = _C_GATE + _W_GATE


def _rms_mod(x, g, sc, sh):
    ms = jnp.mean(x * x, axis=-1, keepdims=True)
    return (x * lax.rsqrt(ms + EPS) * g) * (1.0 + sc) + sh


def _inproj_kernel(x_ref, sc_ref, sh_ref, g_ref, w_ref,
                   hy_ref, q_ref, k_ref, v_ref, gu_ref, gv_ref, gate_ref):
    hb = _rms_mod(x_ref[...], g_ref[...], sc_ref[...], sh_ref[...]).astype(BF16)

    def proj(c0, width):
        return _dot(hb, w_ref[:, c0:c0 + width])

    for j in range(3):
        hy_ref[:, j * 512:(j + 1) * 512] = proj(j * 512, 512).astype(BF16)
    q_ref[...] = proj(_C_Q, _W_Q).astype(BF16)
    k_ref[...] = proj(_C_K, _W_KV).astype(BF16)
    v_ref[...] = proj(_C_V, _W_KV).astype(BF16)
    gu_ref[...] = proj(_C_GU, _W_GM).astype(BF16)
    gv_ref[...] = proj(_C_GV, _W_GM).astype(BF16)
    for j in range(3):
        z = proj(_C_GATE + j * 1024, 1024)
        gate_ref[:, j * 1024:(j + 1) * 1024] = jax.nn.sigmoid(z).astype(BF16)


def _inproj(x, sc, sh, g, w_in_bf16, tm):
    b, s, d = x.shape
    widths = (_W_HY, _W_Q, _W_KV, _W_KV, _W_GM, _W_GM, _W_GATE)
    row = lambda bi, si: (bi, si, 0)
    vec = lambda bi, si: (bi, 0, 0)
    return pl.pallas_call(
        _inproj_kernel,
        grid=(b, s // tm),
        in_specs=[
            pl.BlockSpec((None, tm, d), row),
            pl.BlockSpec((None, 1, d), vec),
            pl.BlockSpec((None, 1, d), vec),
            pl.BlockSpec((1, d), lambda bi, si: (0, 0)),
            _resident((d, IN_COLS), lambda bi, si: (0, 0)),
        ],
        out_specs=[pl.BlockSpec((None, tm, w), row) for w in widths],
        out_shape=[jax.ShapeDtypeStruct((b, s, w), BF16) for w in widths],
        compiler_params=_cparams("parallel", "parallel"),
        name="inproj",
    )(x, sc, sh, g, w_in_bf16)


def _dft_tables(s):
    n = 2 * s
    idx = (np.arange(s, dtype=np.int64)[:, None] * np.arange(s, dtype=np.int64)[None, :]) % n
    ang = idx.astype(np.float64) * (2.0 * np.pi / n)
    return np.cos(ang).astype(np.float32), np.sin(ang).astype(np.float32)


def _filter_tables(s, width):
    f32 = np.float32
    pos = np.arange(s, dtype=f32)
    t = np.linspace(0.0, 1.0, s, dtype=f32)
    w = (f32(2.0 * math.pi) * pos / f32(s)).astype(f32)
    bands = np.linspace(1e-4, HY_POS_BANDS - 1, HY_POS_BANDS, dtype=f32)
    ang = w[:, None] * bands[None, :]
    feats = np.concatenate([t[:, None], np.cos(ang), -np.sin(ang)], axis=-1).astype(f32)
    deltas = np.abs(np.linspace(math.log(HY_DECAY_TARGET) / HY_FAST_DECAY,
                                math.log(HY_DECAY_TARGET) / HY_SLOW_DECAY, width, dtype=f32))
    window = (np.exp(-t[:, None] * deltas[None, :]) + f32(HY_MOD_SHIFT)).astype(f32)
    mirror = (s - np.arange(s)) % s
    feats_pad = np.zeros((s, LANES), f32)
    feats_pad[:, :feats.shape[1]] = feats
    window_rev = window[mirror].copy()
    window_rev[0] = 0.0
    return (np.ascontiguousarray(feats_pad.T), np.ascontiguousarray(feats_pad[mirror].T),
            np.ascontiguousarray(window.T), np.ascontiguousarray(window_rev.T))


def _alt_sign(s):
    lane = lax.broadcasted_iota(jnp.int32, (1, s), 1)
    return jnp.where(lane % 2 == 0, 1.0, -1.0).astype(F32)


def _filter_kernel(ff_ref, fr_ref, wf_ref, wr_ref, f1w_ref, f1b_ref, f1f_ref, f2w_ref, f2b_ref, f2f_ref,
                   f3f_ref, f3b_ref, c_ref, s_ref, p_ref, q_ref, kn_ref, af_ref, ar_ref):
    tr, s = p_ref.shape
    n = 2 * s

    @pl.when(pl.program_id(0) == 0)
    def _():
        def mlp(feats):
            a = jnp.sin(f1f_ref[...] * (_dot(f1w_ref[...], feats, HIGHEST) + f1b_ref[...]))
            return jnp.sin(f2f_ref[...] * (_dot(f2w_ref[...], a, HIGHEST) + f2b_ref[...]))
        af_ref[...] = mlp(ff_ref[...])
        ar_ref[...] = mlp(fr_ref[...])

    h1 = _dot(f3f_ref[...], af_ref[...], HIGHEST) * wf_ref[...]
    h2 = _dot(f3b_ref[...], ar_ref[...], HIGHEST) * wr_ref[...]
    l1 = jnp.sum(jnp.abs(h1), axis=1, keepdims=True) + jnp.sum(jnp.abs(h2), axis=1, keepdims=True)
    k1 = h1 / l1
    k2 = h2 / l1
    kk = jnp.concatenate([k1, k2], axis=0).astype(BF16)
    ck = _dot(kk, c_ref[...])
    sk = _dot(kk, s_ref[...])
    sign = _alt_sign(s)
    kc = ck[:tr] + sign * ck[tr:]
    ks = sk[:tr] + sign * sk[tr:]
    lane = lax.broadcasted_iota(jnp.int32, (1, s), 1)
    wgt = jnp.where(lane == 0, 1.0 / n, 2.0 / n).astype(F32)
    p_ref[...] = kc * wgt
    q_ref[...] = ks * wgt
    kn_ref[...] = jnp.sum(sign * (k1 + k2), axis=1, keepdims=True) * (1.0 / n)


def _hyena_filters(s, width, f1_w, f1_b, f1_freq, f2_w, f2_b, f2_freq, f3_w, cmat, smat, tr=256):
    feats_f, feats_r, win_f, win_r = (jnp.asarray(a) for a in _filter_tables(s, width))
    hid = f1_w.shape[1]
    f1w_t = jnp.zeros((hid, LANES), F32).at[:, :f1_w.shape[0]].set(f1_w.T)
    f3_t = f3_w.T
    n_row = 2 * width
    nrt = n_row // tr
    wpt = width // tr
    col = lambda v: v.reshape(hid, 1)
    full = lambda shape: pl.BlockSpec(shape, lambda j: (0,) * len(shape))
    return pl.pallas_call(
        _filter_kernel,
        grid=(nrt,),
        in_specs=[
            full((LANES, s)), full((LANES, s)),
            pl.BlockSpec((tr, s), lambda j: (j % wpt, 0)),
            pl.BlockSpec((tr, s), lambda j: (j % wpt, 0)),
            full((hid, LANES)), full((hid, 1)), full((hid, 1)),
            full((hid, hid)), full((hid, 1)), full((hid, 1)),
            pl.BlockSpec((tr, hid), lambda j: (j, 0)),
            pl.BlockSpec((tr, hid), lambda j: (j + nrt, 0)),
            _resident((s, s), lambda j: (0, 0)),
            _resident((s, s), lambda j: (0, 0)),
        ],
        out_specs=[
            pl.BlockSpec((tr, s), lambda j: (j, 0)),
            pl.BlockSpec((tr, s), lambda j: (j, 0)),
            pl.BlockSpec((tr, 1), lambda j: (j, 0)),
        ],
        out_shape=[
            jax.ShapeDtypeStruct((n_row, s), F32),
            jax.ShapeDtypeStruct((n_row, s), F32),
            jax.ShapeDtypeStruct((n_row, 1), F32),
        ],
        scratch_shapes=[pltpu.VMEM((hid, s), F32), pltpu.VMEM((hid, s), F32)],
        compiler_params=_cparams("arbitrary"),
        name="hyena_filter",
    )(feats_f, feats_r, win_f, win_r, f1w_t, col(f1_b), col(f1_freq),
      f2_w.T, col(f2_b), col(f2_freq), f3_t, f3_t, cmat, smat)


_HY_CHUNK = 512


def _hyena_kernel(zv_ref, z1_ref, z2_ref, wv_ref, w1_ref, w2_ref, bv_ref, b1_ref, b2_ref,
                  p0_ref, q0_ref, n0_ref, p1_ref, q1_ref, n1_ref, bias_ref, c_ref, s_ref, o_ref,
                  u_ref, x1_ref, x2_ref, yr_ref, yi_ref):
    nb, s, tc = o_ref.shape
    row = lax.broadcasted_iota(jnp.int32, (s, 1), 0)
    sign = _alt_sign(s)
    chunks = [slice(c0, c0 + _HY_CHUNK) for c0 in range(0, s, _HY_CHUNK)]

    def short_conv_t(z_ref, w_ref, b_ref, dst_ref):
        for i in range(nb):
            z = z_ref[i].astype(F32)
            zp = jnp.where(row == 0, 0.0, pltpu.roll(z, 1, 0))
            zn = jnp.where(row == s - 1, 0.0, pltpu.roll(z, s - 1, 0))
            y = zp * w_ref[0:1, :] + z * w_ref[1:2, :] + zn * w_ref[2:3, :] + b_ref[...]
            dst_ref[i * tc:(i + 1) * tc, :] = y.T.astype(BF16)

    def per_batch(t):
        return jnp.concatenate([t] * nb, axis=0)

    def long_conv(p_ref, q_ref, n_ref, bias, x_ref, emit):
        nyq = jnp.sum(sign * u_ref[...].astype(F32), axis=1, keepdims=True) * per_batch(n_ref[...])
        for ch in chunks:
            a = _dot(u_ref[...], c_ref[:, ch])
            b = _dot(u_ref[...], s_ref[:, ch])
            p = per_batch(p_ref[:, ch])
            q = per_batch(q_ref[:, ch])
            yr_ref[:, ch] = (a * p - b * q).astype(BF16)
            yi_ref[:, ch] = (a * q + b * p).astype(BF16)
        for ch in chunks:
            y = _dot(yr_ref[...], c_ref[:, ch]) + _dot(yi_ref[...], s_ref[:, ch])
            y = y + sign[:, ch] * nyq + u_ref[:, ch].astype(F32) * bias
            emit(ch, x_ref[:, ch].astype(F32) * y)

    short_conv_t(zv_ref, wv_ref, bv_ref, u_ref)
    short_conv_t(z1_ref, w1_ref, b1_ref, x1_ref)
    short_conv_t(z2_ref, w2_ref, b2_ref, x2_ref)

    def to_u(ch, y):
        u_ref[:, ch] = y.astype(BF16)

    def to_out(ch, y):
        for i in range(nb):
            o_ref[i, ch, :] = y[i * tc:(i + 1) * tc].T.astype(BF16)

    long_conv(p0_ref, q0_ref, n0_ref, per_batch(bias_ref[:, 0:1]), x1_ref, to_u)
    long_conv(p1_ref, q1_ref, n1_ref, per_batch(bias_ref[:, 1:2]), x2_ref, to_out)


def _hyena(z_hy, conv_w, conv_b, pmat, qmat, nyq, bias_t, cmat, smat, tc=256, nb=2):
    b, s, _ = z_hy.shape
    width = bias_t.shape[0]
    nct = width // tc
    conv_b = conv_b.reshape(1, 3 * width)
    zcol = lambda k: pl.BlockSpec((nb, s, tc), lambda ci, bi, k=k: (bi, 0, ci + k * nct),
                                  pipeline_mode=pl.Buffered(1))
    wcol = lambda rows, k: pl.BlockSpec((rows, tc), lambda ci, bi, k=k: (0, ci + k * nct))
    spec = lambda k: _resident((tc, s), lambda ci, bi, k=k: (ci + k * nct, 0))
    nspec = lambda k: pl.BlockSpec((tc, 1), lambda ci, bi, k=k: (ci + k * nct, 0))
    m = nb * tc
    return pl.pallas_call(
        _hyena_kernel,
        grid=(nct, b // nb),
        in_specs=[
            zcol(0), zcol(1), zcol(2),
            wcol(3, 0), wcol(3, 1), wcol(3, 2),
            wcol(1, 0), wcol(1, 1), wcol(1, 2),
            spec(0), spec(0), nspec(0),
            spec(1), spec(1), nspec(1),
            pl.BlockSpec((tc, 2), lambda ci, bi: (ci, 0)),
            _resident((s, s), lambda ci, bi: (0, 0)),
            _resident((s, s), lambda ci, bi: (0, 0)),
        ],
        out_specs=pl.BlockSpec((nb, s, tc), lambda ci, bi: (bi, 0, ci)),
        out_shape=jax.ShapeDtypeStruct((b, s, width), BF16),
        scratch_shapes=[pltpu.VMEM((m, s), BF16) for _ in range(5)],
        compiler_params=_cparams("arbitrary", "arbitrary"),
        name="hyena",
    )(z_hy, z_hy, z_hy, conv_w, conv_w, conv_w, conv_b, conv_b, conv_b,
      pmat, qmat, nyq, pmat, qmat, nyq, bias_t, cmat, smat)


def _rope_tables(s):
    pos = jnp.arange(s, dtype=F32)
    inv = ROPE_THETA ** (-jnp.arange(0, HEAD_DIM, 2, dtype=F32) / HEAD_DIM)
    ang = pos[:, None] * inv[None, :]
    cos, sin = jnp.cos(ang), jnp.sin(ang)
    reps = LANES // HEAD_DIM
    return (jnp.tile(jnp.concatenate([cos, cos], axis=1), (1, reps)),
            jnp.tile(jnp.concatenate([-sin, sin], axis=1), (1, reps)))


def _head_mean_matrix(width):
    i = np.arange(width)
    return jnp.asarray((i[:, None] // HEAD_DIM == i[None, :] // HEAD_DIM).astype(np.float32) / HEAD_DIM)


def _norm_rope(z, g, hm, cos, sin):
    width = z.shape[1]
    hi, lo = _split_bf16(z * z)
    ms = _dot(hi, hm) + _dot(lo, hm)
    zn = z * lax.rsqrt(ms + EPS) * g
    lane = lax.broadcasted_iota(jnp.int32, (1, width), 1)
    half = HEAD_DIM // 2
    partner = jnp.where(lane % HEAD_DIM < half, pltpu.roll(zn, width - half, 1), pltpu.roll(zn, half, 1))
    return zn * cos + partner * sin


_ATT_PREP_ROWS = 256


def _attn_kernel(sink_ref, q_ref, k_ref, v_ref, qg_ref, kg_ref, hmq_ref, hmk_ref, cos_ref, sin_ref,
                 o_ref, qr_ref, kp_ref, vp_ref):
    s, wq = q_ref.shape
    blk = ATT_BLOCK
    span = 3 * blk
    reps = wq // LANES
    lane = lax.broadcasted_iota(jnp.int32, (1, LANES), 1)
    low = lane < HEAD_DIM

    def prep(c, carry):
        rows = pl.ds(pl.multiple_of(c * _ATT_PREP_ROWS, _ATT_PREP_ROWS), _ATT_PREP_ROWS)
        cos = cos_ref[rows, :]
        sin = sin_ref[rows, :]
        q = _norm_rope(q_ref[rows, :].astype(F32), qg_ref[...], hmq_ref[...],
                       jnp.concatenate([cos] * reps, axis=1), jnp.concatenate([sin] * reps, axis=1))
        qr_ref[rows, :] = (q * (HEAD_DIM ** -0.5)).astype(BF16)
        k = _norm_rope(k_ref[rows, :].astype(F32), kg_ref[...], hmk_ref[...], cos, sin)
        v = v_ref[rows, :].astype(F32)
        for src, dst in ((k, kp_ref), (v, vp_ref)):
            swapped = pltpu.roll(src, HEAD_DIM, 1)
            dst[0, 0, rows, 0:LANES] = jnp.where(low, src, 0.0).astype(BF16)
            dst[0, 1, rows, 0:LANES] = jnp.where(low, 0.0, swapped).astype(BF16)
            dst[1, 0, rows, 0:LANES] = jnp.where(low, swapped, 0.0).astype(BF16)
            dst[1, 1, rows, 0:LANES] = jnp.where(low, 0.0, src).astype(BF16)
        ones_lo = jnp.broadcast_to(jnp.where(low, 1.0, 0.0), (_ATT_PREP_ROWS, LANES)).astype(BF16)
        ones_hi = jnp.broadcast_to(jnp.where(low, 0.0, 1.0), (_ATT_PREP_ROWS, LANES)).astype(BF16)
        for kh in range(N_KV_HEADS):
            vp_ref[kh, 0, rows, LANES:2 * LANES] = ones_lo
            vp_ref[kh, 1, rows, LANES:2 * LANES] = ones_hi
        return carry

    lax.fori_loop(0, s // _ATT_PREP_ROWS, prep, 0)

    def block(i, carry):
        q0 = pl.multiple_of(i * blk, blk)
        k0 = pl.multiple_of(jnp.clip(q0 - blk, 0, s - span), blk)
        qrows = pl.ds(q0, blk)
        krows = pl.ds(k0, span)
        rel = (lax.broadcasted_iota(jnp.int32, (blk, span), 1) + (k0 - q0)
               - lax.broadcasted_iota(jnp.int32, (blk, span), 0))
        valid = jnp.abs(rel) <= WINDOW
        valid = jnp.concatenate([valid, valid], axis=0)
        for kh in range(N_KV_HEADS):
            c0 = kh * 2 * LANES
            qs = jnp.concatenate([qr_ref[qrows, c0:c0 + LANES], qr_ref[qrows, c0 + LANES:c0 + 2 * LANES]],
                                 axis=0)
            es, sinks = [], []
            for r in range(2):
                sc = _dot_nt(qs, kp_ref[kh, r, krows, :])
                sc = jnp.where(valid, sc, NEG_INF)
                sink = jnp.concatenate([jnp.full((blk, LANES), sink_ref[4 * kh + r], F32),
                                        jnp.full((blk, LANES), sink_ref[4 * kh + 2 + r], F32)], axis=0)
                m = jnp.maximum(jnp.max(sc, axis=-1, keepdims=True), sink)
                es.append(jnp.exp(sc - jnp.concatenate([m] * (span // LANES), axis=1)).astype(BF16))
                sinks.append(jnp.exp(sink - m))
            od = _dot(es[0], vp_ref[kh, 0, krows, :]) + _dot(es[1], vp_ref[kh, 1, krows, :])
            o = od[:, :LANES] / (od[:, LANES:] + jnp.where(low, sinks[0], sinks[1]))
            o_ref[qrows, c0:c0 + LANES] = o[:blk].astype(BF16)
            o_ref[qrows, c0 + LANES:c0 + 2 * LANES] = o[blk:].astype(BF16)
        return carry

    lax.fori_loop(0, s // blk, block, 0, unroll=2)


def _attention(zq, zk, zv, q_g, k_g, sink, cos_t, sin_t):
    b, s, wq = zq.shape
    wk = zk.shape[2]
    assert wq == N_Q_HEADS * HEAD_DIM and wk == N_KV_HEADS * HEAD_DIM == LANES
    full = lambda shape: pl.BlockSpec(shape, lambda bi: (0,) * len(shape))
    per_b = lambda w: pl.BlockSpec((None, s, w), lambda bi: (bi, 0, 0))
    return pl.pallas_call(
        _attn_kernel,
        grid=(b,),
        in_specs=[
            pl.BlockSpec(memory_space=pltpu.SMEM),
            per_b(wq), per_b(wk), per_b(wk),
            full((1, wq)), full((1, wk)), full((wq, wq)), full((wk, wk)),
            full((s, LANES)), full((s, LANES)),
        ],
        out_specs=per_b(wq),
        out_shape=jax.ShapeDtypeStruct((b, s, wq), BF16),
        scratch_shapes=[pltpu.VMEM((s, wq), BF16),
                        pltpu.VMEM((N_KV_HEADS, 2, s, LANES), BF16),
                        pltpu.VMEM((N_KV_HEADS, 2, s, 2 * LANES), BF16)],
        compiler_params=_cparams("parallel"),
        name="attn",
    )(sink, zq, zk, zv, jnp.tile(q_g, wq // HEAD_DIM).reshape(1, wq),
      jnp.tile(k_g, wk // HEAD_DIM).reshape(1, wk),
      _head_mean_matrix(wq).astype(BF16), _head_mean_matrix(wk).astype(BF16), cos_t, sin_t)


def _gelu(x):
    return 0.5 * x * (1.0 + lax.erf(x * (1.0 / math.sqrt(2.0))))


def _gmlp_kernel(zu_ref, zv_ref, g_ref, b_ref, ws_ref, pb_ref, o_ref):
    tm, width = o_ref.shape
    gch = width // GM_GROUPS
    u = _gelu(zu_ref[...].astype(F32))
    v = _gelu(zv_ref[...].astype(F32))
    mu = jnp.mean(v, axis=-1, keepdims=True)
    var = jnp.mean(jnp.square(v - mu), axis=-1, keepdims=True)
    v = ((v - mu) * lax.rsqrt(var + EPS) * g_ref[...] + b_ref[...]).astype(BF16)
    for c in range(tm // GM_CHUNK):
        r0 = c * GM_CHUNK
        cols = []
        for g in range(GM_GROUPS):
            sv = _dot(ws_ref[g], v[r0:r0 + GM_CHUNK, g * gch:(g + 1) * gch]) + pb_ref[:, g:g + 1]
            cols.append(sv)
        sv = jnp.concatenate(cols, axis=1)
        o_ref[r0:r0 + GM_CHUNK, :] = (u[r0:r0 + GM_CHUNK, :] * sv).astype(BF16)


def _gmlp(zu, zv, ln_g, ln_b, ws_bf16, pos_bias_t, tm):
    b, s, width = zu.shape
    row = lambda bi, si: (bi, si, 0)
    full = lambda shape: pl.BlockSpec(shape, lambda bi, si: (0,) * len(shape))
    return pl.pallas_call(
        _gmlp_kernel,
        grid=(b, s // tm),
        in_specs=[
            pl.BlockSpec((None, tm, width), row), pl.BlockSpec((None, tm, width), row),
            full((1, width)), full((1, width)),
            full((GM_GROUPS, GM_CHUNK, GM_CHUNK)), full((GM_CHUNK, GM_GROUPS)),
        ],
        out_specs=pl.BlockSpec((None, tm, width), row),
        out_shape=jax.ShapeDtypeStruct((b, s, width), BF16),
        compiler_params=_cparams("parallel", "parallel"),
        name="gmlp",
    )(zu, zv, ln_g.reshape(1, width), ln_b.reshape(1, width), ws_bf16, pos_bias_t)


def _merge_kernel(x_ref, yh_ref, ya_ref, yg_ref, gate_ref, gt_ref, wb_ref, wo_ref,
                  g2_ref, sc2_ref, sh2_ref, wrh_ref, wrl_ref, x1_ref, h2_ref, aff_ref):
    d = x_ref.shape[1]
    n_e = aff_ref.shape[0]
    acc = None
    for j, y_ref in enumerate((yh_ref, ya_ref, yg_ref)):
        br = _dot(y_ref[...], wb_ref[j])
        term = gate_ref[:, j * d:(j + 1) * d].astype(F32) * br
        acc = term if acc is None else acc + term
    mix = _dot(acc.astype(BF16), wo_ref[...])
    x1 = x_ref[...] + gt_ref[...] * mix
    x1_ref[...] = x1
    h2 = _rms_mod(x1, g2_ref[...], sc2_ref[...], sh2_ref[...])
    h2_ref[...] = h2.astype(BF16)
    hi, lo = _split_bf16(h2)
    logits = _dot(hi, wrh_ref[...]) + _dot(lo, wrh_ref[...]) + _dot(hi, wrl_ref[...])
    logits = logits.T[:n_e]
    m = jnp.max(logits, axis=0, keepdims=True)
    e = jnp.exp(logits - m)
    aff_ref[...] = e / jnp.sum(e, axis=0, keepdims=True)


def _merge(x, y_hy, y_at, y_gm, gates, gt1, wb_bf16, wo_bf16, g2, sc2, sh2, w_router, tm):
    b, s, d = x.shape
    width = y_hy.shape[2]
    n_e = w_router.shape[1]
    wr = jnp.zeros((d, LANES), F32).at[:, :n_e].set(w_router)
    wr_hi = wr.astype(BF16)
    wr_lo = (wr - wr_hi.astype(F32)).astype(BF16)
    row = lambda bi, si: (bi, si, 0)
    vec = lambda bi, si: (bi, 0, 0)
    full = lambda shape: pl.BlockSpec(shape, lambda bi, si: (0,) * len(shape))
    return pl.pallas_call(
        _merge_kernel,
        grid=(b, s // tm),
        in_specs=[
            pl.BlockSpec((None, tm, d), row),
            pl.BlockSpec((None, tm, width), row), pl.BlockSpec((None, tm, width), row),
            pl.BlockSpec((None, tm, width), row),
            pl.BlockSpec((None, tm, 3 * d), row),
            pl.BlockSpec((None, 1, d), vec),
            _resident((3, width, d), lambda bi, si: (0, 0, 0)),
            _resident((d, d), lambda bi, si: (0, 0)),
            full((1, d)),
            pl.BlockSpec((None, 1, d), vec), pl.BlockSpec((None, 1, d), vec),
            full((d, LANES)), full((d, LANES)),
        ],
        out_specs=[
            pl.BlockSpec((None, tm, d), row),
            pl.BlockSpec((None, tm, d), row),
            pl.BlockSpec((None, n_e, tm), lambda bi, si: (bi, 0, si)),
        ],
        out_shape=[
            jax.ShapeDtypeStruct((b, s, d), F32),
            jax.ShapeDtypeStruct((b, s, d), BF16),
            jax.ShapeDtypeStruct((b, n_e, s), F32),
        ],
        compiler_params=_cparams("parallel", "parallel"),
        name="merge",
    )(x, y_hy, y_at, y_gm, gates, gt1, wb_bf16, wo_bf16, g2, sc2, sh2, wr_hi, wr_lo)


def _prefix_count(x):
    n = x.shape[-1]
    lane = lax.broadcasted_iota(jnp.int32, x.shape, x.ndim - 1)
    shift = 1
    while shift < n:
        x = x + jnp.where(lane >= shift, pltpu.roll(x, shift, x.ndim - 1), 0.0)
        shift *= 2
    return x


_REFINE_STEPS = 30


def _select_kernel(aff_ref, slot_ref, slot_t_ref, *, cap):
    aff = aff_ref[...]
    n_e = aff.shape[0]

    def count_ge(t):
        return jnp.sum(jnp.where(aff >= t, 1.0, 0.0), axis=-1, keepdims=True)

    def coarse(j, cur):
        cand = cur | (jnp.int32(1) << (30 - j))
        return jnp.where(count_ge(pltpu.bitcast(cand, F32)) >= cap, cand, cur)

    bits = lax.fori_loop(0, 31, coarse, jnp.zeros((n_e, 1), jnp.int32))
    lo = pltpu.bitcast(bits, F32)
    hi = pltpu.bitcast(bits + 1, F32)

    def refine(j, lohi):
        lo, hi = lohi
        mid = lo + 0.5 * (hi - lo)
        ok = count_ge(mid) >= cap
        return jnp.where(ok, mid, lo), jnp.where(ok, hi, mid)

    lo, hi = lax.fori_loop(0, _REFINE_STEPS, refine, (lo, hi))
    thr = jnp.min(jnp.where(aff >= lo, aff, jnp.inf), axis=-1, keepdims=True)

    above = jnp.where(aff > thr, 1.0, 0.0)
    tied = jnp.where(aff == thr, 1.0, 0.0)
    need = cap - jnp.sum(above, axis=-1, keepdims=True)
    tie_rank = _prefix_count(tied) - tied
    chosen = above + tied * jnp.where(tie_rank < need, 1.0, 0.0)
    slot = _prefix_count(chosen) - chosen
    slot = jnp.where(chosen > 0.0, slot, -1.0).astype(jnp.int32)
    slot_ref[...] = slot
    pad = jnp.full((LANES - n_e, slot.shape[1]), -1, jnp.int32)
    slot_t_ref[...] = jnp.concatenate([slot, pad], axis=0).T


def _select(aff, cap):
    b, n_e, s = aff.shape
    return pl.pallas_call(
        functools.partial(_select_kernel, cap=cap),
        grid=(b,),
        in_specs=[pl.BlockSpec((None, n_e, s), lambda bi: (bi, 0, 0))],
        out_specs=[
            pl.BlockSpec((None, n_e, s), lambda bi: (bi, 0, 0)),
            pl.BlockSpec((None, s, LANES), lambda bi: (bi, 0, 0)),
        ],
        out_shape=[
            jax.ShapeDtypeStruct((b, n_e, s), jnp.int32),
            jax.ShapeDtypeStruct((b, s, LANES), jnp.int32),
        ],
        compiler_params=_cparams("parallel"),
        name="select",
    )(aff)


def _gather_kernel(slot_ref, aff_ref, h_ref, xg_ref, gate_ref):
    cap = xg_ref.shape[0]
    s = h_ref.shape[0]
    hit = slot_ref[...] == lax.broadcasted_iota(jnp.int32, (cap, s), 0)
    onehot = jnp.where(hit, 1.0, 0.0).astype(BF16)
    xg_ref[...] = _dot(onehot, h_ref[...]).astype(BF16)
    gate_ref[...] = jnp.sum(jnp.where(hit, aff_ref[...], 0.0), axis=-1, keepdims=True)


def _gather(slot, aff, h2, cap):
    b, n_e, s = slot.shape
    d = h2.shape[2]
    slot4 = slot.reshape(b, n_e, 1, s)
    aff4 = aff.reshape(b, n_e, 1, s)
    return pl.pallas_call(
        _gather_kernel,
        grid=(b, n_e),
        in_specs=[
            pl.BlockSpec((None, None, 1, s), lambda bi, e: (bi, e, 0, 0)),
            pl.BlockSpec((None, None, 1, s), lambda bi, e: (bi, e, 0, 0)),
            pl.BlockSpec((None, s, d), lambda bi, e: (bi, 0, 0)),
        ],
        out_specs=[
            pl.BlockSpec((None, None, cap, d), lambda bi, e: (bi, e, 0, 0)),
            pl.BlockSpec((None, None, cap, 1), lambda bi, e: (bi, e, 0, 0)),
        ],
        out_shape=[
            jax.ShapeDtypeStruct((b, n_e, cap, d), BF16),
            jax.ShapeDtypeStruct((b, n_e, cap, 1), F32),
        ],
        compiler_params=_cparams("parallel", "arbitrary"),
        name="gather",
    )(slot4, aff4, h2)


def _expert_kernel(xg_ref, gate_ref, wg_ref, wu_ref, wd_ref, y_ref, *, f_chunk):
    nb, cap, d = xg_ref.shape
    x = xg_ref[...].reshape(nb * cap, d)
    f = wg_ref.shape[1]
    acc = None
    for c in range(f // f_chunk):
        cs = slice(c * f_chunk, (c + 1) * f_chunk)
        a = _dot(x, wg_ref[:, cs])
        u = _dot(x, wu_ref[:, cs])
        act = (a * jax.nn.sigmoid(a) * u).astype(BF16)
        part = _dot(act, wd_ref[cs, :])
        acc = part if acc is None else acc + part
    y = acc * gate_ref[...].reshape(nb * cap, 1)
    y_ref[...] = y.reshape(nb, cap, d).astype(BF16)


def _experts(xg, gate, wg, wu, wd, nb, f_chunk):
    b, n_e, cap, d = xg.shape
    f = wg.shape[2]
    return pl.pallas_call(
        functools.partial(_expert_kernel, f_chunk=f_chunk),
        grid=(n_e, b // nb),
        in_specs=[
            pl.BlockSpec((nb, None, cap, d), lambda e, bg: (bg, e, 0, 0)),
            pl.BlockSpec((nb, None, cap, 1), lambda e, bg: (bg, e, 0, 0)),
            pl.BlockSpec((None, d, f), lambda e, bg: (e, 0, 0)),
            pl.BlockSpec((None, d, f), lambda e, bg: (e, 0, 0)),
            pl.BlockSpec((None, f, d), lambda e, bg: (e, 0, 0)),
        ],
        out_specs=pl.BlockSpec((nb, None, cap, d), lambda e, bg: (bg, e, 0, 0)),
        out_shape=jax.ShapeDtypeStruct((b, n_e, cap, d), BF16),
        compiler_params=_cparams("arbitrary", "arbitrary"),
        name="expert",
    )(xg, gate, wg, wu, wd)


def _combine_kernel(x_ref, gt_ref, slot_t_ref, y_ref, o_ref):
    tm = x_ref.shape[0]
    n_e, cap, d = y_ref.shape
    slot_t = slot_t_ref[...]
    lane = lax.broadcasted_iota(jnp.int32, (tm, cap), 1)
    acc = jnp.zeros((tm, d), F32)
    for e in range(n_e):
        onehot = jnp.where(slot_t[:, e:e + 1] == lane, 1.0, 0.0).astype(BF16)
        acc = acc + _dot(onehot, y_ref[e])
    o_ref[...] = x_ref[...] + gt_ref[...] * acc


def _combine(x1, gt2, slot_t, y, tm):
    b, s, d = x1.shape
    _, n_e, cap, _ = y.shape
    return pl.pallas_call(
        _combine_kernel,
        grid=(b, s // tm),
        in_specs=[
            pl.BlockSpec((None, tm, d), lambda bi, si: (bi, si, 0)),
            pl.BlockSpec((None, 1, d), lambda bi, si: (bi, 0, 0)),
            pl.BlockSpec((None, tm, LANES), lambda bi, si: (bi, si, 0)),
            pl.BlockSpec((None, n_e, cap, d), lambda bi, si: (bi, 0, 0, 0)),
        ],
        out_specs=pl.BlockSpec((None, tm, d), lambda bi, si: (bi, si, 0)),
        out_shape=jax.ShapeDtypeStruct((b, s, d), F32),
        compiler_params=_cparams("parallel", "arbitrary"),
        name="combine",
    )(x1, gt2, slot_t, y)


def kernel(x, c, w_mod, b_mod, norm1_g, norm2_g, w_in, hy_conv_w, hy_conv_b, hy_f1_w, hy_f1_b, hy_f1_freq,
           hy_f2_w, hy_f2_b, hy_f2_freq, hy_f3_w, hy_bias, q_norm_g, k_norm_g, attn_sink, gm_ln_g, gm_ln_b,
           gm_ws, gm_b, w_branch, w_out, w_router, w_e_gate, w_e_up, w_e_down):
    b, s, d = x.shape
    depth = w_mod.shape[0]
    width = hy_bias.shape[2]
    cap = EC_CAPACITY * s // N_EXPERTS
    assert w_in.shape[2] == IN_COLS and s % 512 == 0 and b % 2 == 0
    tm = 512

    cmat_np, smat_np = _dft_tables(s)
    cmat = jnp.asarray(cmat_np).astype(BF16)
    smat = jnp.asarray(smat_np).astype(BF16)
    cos_t, sin_t = _rope_tables(s)
    mod = _modulation(c, w_mod, b_mod)

    for l in range(depth):
        sh1, sc1, gt1, sh2, sc2, gt2 = (mod[l, j] for j in range(6))
        z_hy, z_q, z_k, z_v, z_gu, z_gv, gates = _inproj(
            x, sc1, sh1, norm1_g[l].reshape(1, d), w_in[l].astype(BF16), tm)
        pmat, qmat, nyq = _hyena_filters(s, width, hy_f1_w[l], hy_f1_b[l], hy_f1_freq[l], hy_f2_w[l],
                                         hy_f2_b[l], hy_f2_freq[l], hy_f3_w[l], cmat, smat)
        y_hy = _hyena(z_hy, hy_conv_w[l], hy_conv_b[l], pmat, qmat, nyq, hy_bias[l].T, cmat, smat)
        y_at = _attention(z_q, z_k, z_v, q_norm_g[l], k_norm_g[l], attn_sink[l], cos_t, sin_t)
        y_gm = _gmlp(z_gu, z_gv, gm_ln_g[l], gm_ln_b[l], gm_ws[l].astype(BF16), gm_b[l].T, tm)
        x1, h2, aff = _merge(x, y_hy, y_at, y_gm, gates, gt1, w_branch[l].astype(BF16),
                             w_out[l].astype(BF16), norm2_g[l].reshape(1, d), sc2, sh2, w_router[l], tm)
        slot, slot_t = _select(aff, cap)
        xg, gate = _gather(slot, aff, h2, cap)
        y = _experts(xg, gate, w_e_gate[l].astype(BF16), w_e_up[l].astype(BF16),
                     w_e_down[l].astype(BF16), nb=2, f_chunk=1024)
        x = _combine(x1, gt2, slot_t, y, tm)
    return x
```

```python
import functools
import math

import numpy as np
import jax
import jax.numpy as jnp
from jax import lax
from jax.experimental import pallas as pl
from jax.experimental.pallas import tpu as pltpu

F32 = jnp.float32
BF16 = jnp.bfloat16
HIGHEST = lax.Precision.HIGHEST

HEAD_DIM = 64
N_Q_HEADS = 8
N_KV_HEADS = 2
WINDOW = 128
ATT_BLOCK = 128
ROPE_THETA = 10000.0
HY_POS_BANDS = 8
HY_DECAY_TARGET = 1e-2
HY_FAST_DECAY = 0.3
HY_SLOW_DECAY = 1.5
HY_MOD_SHIFT = 0.05
GM_GROUPS = 4
GM_CHUNK = 128
N_EXPERTS = 16
EC_CAPACITY = 2
EPS = 1e-6
NEG_INF = -1e30

V7X_VMEM_LIMIT_BYTES = 56 * 1024 * 1024
LANES = 128


def _cparams(*sem):
    return pltpu.CompilerParams(dimension_semantics=sem, vmem_limit_bytes=V7X_VMEM_LIMIT_BYTES)


def _resident(shape, index_map):
    return pl.BlockSpec(shape, index_map, pipeline_mode=pl.Buffered(1))


def _dot(a, b, precision=None):
    return jnp.dot(a, b, preferred_element_type=F32, precision=precision)


def _dot_nt(a, b):
    return lax.dot_general(a, b, (((1,), (1,)), ((), ())), preferred_element_type=F32)


def _split_bf16(x):
    hi = x.astype(BF16)
    return hi, (x - hi.astype(F32)).astype(BF16)


def _mod_kernel(c_ref, w_ref, b_ref, o_ref):
    c = c_ref[...]
    cond = c * jax.nn.sigmoid(c)
    o_ref[...] = _dot(cond, w_ref[...], HIGHEST) + b_ref[...]


def _modulation(c, w_mod, b_mod):
    depth, d, _ = w_mod.shape
    b = c.shape[0]
    out = pl.pallas_call(
        _mod_kernel,
        grid=(depth, 6),
        in_specs=[
            pl.BlockSpec((b, d), lambda l, j: (0, 0)),
            pl.BlockSpec((None, d, d), lambda l, j: (l, 0, j)),
            pl.BlockSpec((None, None, 1, d), lambda l, j: (l, j, 0, 0)),
        ],
        out_specs=pl.BlockSpec((None, None, b, d), lambda l, j: (l, j, 0, 0)),
        out_shape=jax.ShapeDtypeStruct((depth, 6, b, d), F32),
        compiler_params=_cparams("arbitrary", "arbitrary"),
        name="mod",
    )(c, w_mod, b_mod.reshape(depth, 6, 1, d))
    return out.reshape(depth, 6, b, 1, d)


_W_HY, _W_Q, _W_KV, _W_GM, _W_GATE = 1536, 512, 128, 512, 3072
_C_Q = _W_HY
_C_K = _C_Q + _W_Q
_C_V = _C_K + _W_KV
_C_GU = _C_V + _W_KV
_C_GV = _C_GU + _W_GM
_C_GATE = _C_GV + _W_GM
IN_COLS = _C_GATE + _W_GATE


def _rms_mod(x, g, sc, sh):
    ms = jnp.mean(x * x, axis=-1, keepdims=True)
    return (x * lax.rsqrt(ms + EPS) * g) * (1.0 + sc) + sh


def _inproj_kernel(x_ref, sc_ref, sh_ref, g_ref, w_ref,
                   hy_ref, q_ref, k_ref, v_ref, gu_ref, gv_ref, gate_ref):
    hb = _rms_mod(x_ref[...], g_ref[...], sc_ref[...], sh_ref[...]).astype(BF16)

    def proj(c0, width):
        return _dot(hb, w_ref[:, c0:c0 + width])

    for j in range(3):
        hy_ref[:, j * 512:(j + 1) * 512] = proj(j * 512, 512).astype(BF16)
    q_ref[...] = proj(_C_Q, _W_Q).astype(BF16)
    k_ref[...] = proj(_C_K, _W_KV).astype(BF16)
    v_ref[...] = proj(_C_V, _W_KV).astype(BF16)
    gu_ref[...] = proj(_C_GU, _W_GM).astype(BF16)
    gv_ref[...] = proj(_C_GV, _W_GM).astype(BF16)
    for j in range(3):
        z = proj(_C_GATE + j * 1024, 1024)
        gate_ref[:, j * 1024:(j + 1) * 1024] = jax.nn.sigmoid(z).astype(BF16)


def _inproj(x, sc, sh, g, w_in_bf16, tm):
    b, s, d = x.shape
    widths = (_W_HY, _W_Q, _W_KV, _W_KV, _W_GM, _W_GM, _W_GATE)
    row = lambda bi, si: (bi, si, 0)
    vec = lambda bi, si: (bi, 0, 0)
    return pl.pallas_call(
        _inproj_kernel,
        grid=(b, s // tm),
        in_specs=[
            pl.BlockSpec((None, tm, d), row),
            pl.BlockSpec((None, 1, d), vec),
            pl.BlockSpec((None, 1, d), vec),
            pl.BlockSpec((1, d), lambda bi, si: (0, 0)),
            _resident((d, IN_COLS), lambda bi, si: (0, 0)),
        ],
        out_specs=[pl.BlockSpec((None, tm, w), row) for w in widths],
        out_shape=[jax.ShapeDtypeStruct((b, s, w), BF16) for w in widths],
        compiler_params=_cparams("parallel", "parallel"),
        name="inproj",
    )(x, sc, sh, g, w_in_bf16)


def _dft_tables(s):
    n = 2 * s
    idx = (np.arange(s, dtype=np.int64)[:, None] * np.arange(s, dtype=np.int64)[None, :]) % n
    ang = idx.astype(np.float64) * (2.0 * np.pi / n)
    return np.cos(ang).astype(np.float32), np.sin(ang).astype(np.float32)


def _filter_tables(s, width):
    f32 = np.float32
    pos = np.arange(s, dtype=f32)
    t = np.linspace(0.0, 1.0, s, dtype=f32)
    w = (f32(2.0 * math.pi) * pos / f32(s)).astype(f32)
    bands = np.linspace(1e-4, HY_POS_BANDS - 1, HY_POS_BANDS, dtype=f32)
    ang = w[:, None] * bands[None, :]
    feats = np.concatenate([t[:, None], np.cos(ang), -np.sin(ang)], axis=-1).astype(f32)
    deltas = np.abs(np.linspace(math.log(HY_DECAY_TARGET) / HY_FAST_DECAY,
                                math.log(HY_DECAY_TARGET) / HY_SLOW_DECAY, width, dtype=f32))
    window = (np.exp(-t[:, None] * deltas[None, :]) + f32(HY_MOD_SHIFT)).astype(f32)
    mirror = (s - np.arange(s)) % s
    feats_pad = np.zeros((s, LANES), f32)
    feats_pad[:, :feats.shape[1]] = feats
    window_rev = window[mirror].copy()
    window_rev[0] = 0.0
    return (np.ascontiguousarray(feats_pad.T), np.ascontiguousarray(feats_pad[mirror].T),
            np.ascontiguousarray(window.T), np.ascontiguousarray(window_rev.T))


def _alt_sign(s):
    lane = lax.broadcasted_iota(jnp.int32, (1, s), 1)
    return jnp.where(lane % 2 == 0, 1.0, -1.0).astype(F32)


def _filter_kernel(ff_ref, fr_ref, wf_ref, wr_ref, f1w_ref, f1b_ref, f1f_ref, f2w_ref, f2b_ref, f2f_ref,
                   f3f_ref, f3b_ref, c_ref, s_ref, p_ref, q_ref, kn_ref, af_ref, ar_ref):
    tr, s = p_ref.shape
    n = 2 * s

    @pl.when(pl.program_id(0) == 0)
    def _():
        def mlp(feats):
            a = jnp.sin(f1f_ref[...] * (_dot(f1w_ref[...], feats, HIGHEST) + f1b_ref[...]))
            return jnp.sin(f2f_ref[...] * (_dot(f2w_ref[...], a, HIGHEST) + f2b_ref[...]))
        af_ref[...] = mlp(ff_ref[...])
        ar_ref[...] = mlp(fr_ref[...])

    h1 = _dot(f3f_ref[...], af_ref[...], HIGHEST) * wf_ref[...]
    h2 = _dot(f3b_ref[...], ar_ref[...], HIGHEST) * wr_ref[...]
    l1 = jnp.sum(jnp.abs(h1), axis=1, keepdims=True) + jnp.sum(jnp.abs(h2), axis=1, keepdims=True)
    k1 = h1 / l1
    k2 = h2 / l1
    kk = jnp.concatenate([k1, k2], axis=0).astype(BF16)
    ck = _dot(kk, c_ref[...])
    sk = _dot(kk, s_ref[...])
    sign = _alt_sign(s)
    kc = ck[:tr] + sign * ck[tr:]
    ks = sk[:tr] + sign * sk[tr:]
    lane = lax.broadcasted_iota(jnp.int32, (1, s), 1)
    wgt = jnp.where(lane == 0, 1.0 / n, 2.0 / n).astype(F32)
    p_ref[...] = kc * wgt
    q_ref[...] = ks * wgt
    kn_ref[...] = jnp.sum(sign * (k1 + k2), axis=1, keepdims=True) * (1.0 / n)


def _hyena_filters(s, width, f1_w, f1_b, f1_freq, f2_w, f2_b, f2_freq, f3_w, cmat, smat, tr=256):
    feats_f, feats_r, win_f, win_r = (jnp.asarray(a) for a in _filter_tables(s, width))
    hid = f1_w.shape[1]
    f1w_t = jnp.zeros((hid, LANES), F32).at[:, :f1_w.shape[0]].set(f1_w.T)
    f3_t = f3_w.T
    n_row = 2 * width
    nrt = n_row // tr
    wpt = width // tr
    col = lambda v: v.reshape(hid, 1)
    full = lambda shape: pl.BlockSpec(shape, lambda j: (0,) * len(shape))
    return pl.pallas_call(
        _filter_kernel,
        grid=(nrt,),
        in_specs=[
            full((LANES, s)), full((LANES, s)),
            pl.BlockSpec((tr, s), lambda j: (j % wpt, 0)),
            pl.BlockSpec((tr, s), lambda j: (j % wpt, 0)),
            full((hid, LANES)), full((hid, 1)), full((hid, 1)),
            full((hid, hid)), full((hid, 1)), full((hid, 1)),
            pl.BlockSpec((tr, hid), lambda j: (j, 0)),
            pl.BlockSpec((tr, hid), lambda j: (j + nrt, 0)),
            _resident((s, s), lambda j: (0, 0)),
            _resident((s, s), lambda j: (0, 0)),
        ],
        out_specs=[
            pl.BlockSpec((tr, s), lambda j: (j, 0)),
            pl.BlockSpec((tr, s), lambda j: (j, 0)),
            pl.BlockSpec((tr, 1), lambda j: (j, 0)),
        ],
        out_shape=[
            jax.ShapeDtypeStruct((n_row, s), F32),
            jax.ShapeDtypeStruct((n_row, s), F32),
            jax.ShapeDtypeStruct((n_row, 1), F32),
        ],
        scratch_shapes=[pltpu.VMEM((hid, s), F32), pltpu.VMEM((hid, s), F32)],
        compiler_params=_cparams("arbitrary"),
        name="hyena_filter",
    )(feats_f, feats_r, win_f, win_r, f1w_t, col(f1_b), col(f1_freq),
      f2_w.T, col(f2_b), col(f2_freq), f3_t, f3_t, cmat, smat)


_HY_CHUNK = 512


def _hyena_kernel(zv_ref, z1_ref, z2_ref, wv_ref, w1_ref, w2_ref, bv_ref, b1_ref, b2_ref,
                  p0_ref, q0_ref, n0_ref, p1_ref, q1_ref, n1_ref, bias_ref, c_ref, s_ref, o_ref,
                  u_ref, x1_ref, x2_ref, yr_ref, yi_ref):
    nb, s, tc = o_ref.shape
    row = lax.broadcasted_iota(jnp.int32, (s, 1), 0)
    sign = _alt_sign(s)
    chunks = [slice(c0, c0 + _HY_CHUNK) for c0 in range(0, s, _HY_CHUNK)]

    def short_conv_t(z_ref, w_ref, b_ref, dst_ref):
        for i in range(nb):
            z = z_ref[i].astype(F32)
            zp = jnp.where(row == 0, 0.0, pltpu.roll(z, 1, 0))
            zn = jnp.where(row == s - 1, 0.0, pltpu.roll(z, s - 1, 0))
            y = zp * w_ref[0:1, :] + z * w_ref[1:2, :] + zn * w_ref[2:3, :] + b_ref[...]
            dst_ref[i * tc:(i + 1) * tc, :] = y.T.astype(BF16)

    def per_batch(t):
        return jnp.concatenate([t] * nb, axis=0)

    def long_conv(p_ref, q_ref, n_ref, bias, x_ref, emit):
        nyq = jnp.sum(sign * u_ref[...].astype(F32), axis=1, keepdims=True) * per_batch(n_ref[...])
        for ch in chunks:
            a = _dot(u_ref[...], c_ref[:, ch])
            b = _dot(u_ref[...], s_ref[:, ch])
            p = per_batch(p_ref[:, ch])
            q = per_batch(q_ref[:, ch])
            yr_ref[:, ch] = (a * p - b * q).astype(BF16)
            yi_ref[:, ch] = (a * q + b * p).astype(BF16)
        for ch in chunks:
            y = _dot(yr_ref[...], c_ref[:, ch]) + _dot(yi_ref[...], s_ref[:, ch])
            y = y + sign[:, ch] * nyq + u_ref[:, ch].astype(F32) * bias
            emit(ch, x_ref[:, ch].astype(F32) * y)

    short_conv_t(zv_ref, wv_ref, bv_ref, u_ref)
    short_conv_t(z1_ref, w1_ref, b1_ref, x1_ref)
    short_conv_t(z2_ref, w2_ref, b2_ref, x2_ref)

    def to_u(ch, y):
        u_ref[:, ch] = y.astype(BF16)

    def to_out(ch, y):
        for i in range(nb):
            o_ref[i, ch, :] = y[i * tc:(i + 1) * tc].T.astype(BF16)

    long_conv(p0_ref, q0_ref, n0_ref, per_batch(bias_ref[:, 0:1]), x1_ref, to_u)
    long_conv(p1_ref, q1_ref, n1_ref, per_batch(bias_ref[:, 1:2]), x2_ref, to_out)


def _hyena(z_hy, conv_w, conv_b, pmat, qmat, nyq, bias_t, cmat, smat, tc=256, nb=2):
    b, s, _ = z_hy.shape
    width = bias_t.shape[0]
    nct = width // tc
    conv_b = conv_b.reshape(1, 3 * width)
    zcol = lambda k: pl.BlockSpec((nb, s, tc), lambda ci, bi, k=k: (bi, 0, ci + k * nct),
                                  pipeline_mode=pl.Buffered(1))
    wcol = lambda rows, k: pl.BlockSpec((rows, tc), lambda ci, bi, k=k: (0, ci + k * nct))
    spec = lambda k: _resident((tc, s), lambda ci, bi, k=k: (ci + k * nct, 0))
    nspec = lambda k: pl.BlockSpec((tc, 1), lambda ci, bi, k=k: (ci + k * nct, 0))
    m = nb * tc
    return pl.pallas_call(
        _hyena_kernel,
        grid=(nct, b // nb),
        in_specs=[
            zcol(0), zcol(1), zcol(2),
            wcol(3, 0), wcol(3, 1), wcol(3, 2),
            wcol(1, 0), wcol(1, 1), wcol(1, 2),
            spec(0), spec(0), nspec(0),
            spec(1), spec(1), nspec(1),
            pl.BlockSpec((tc, 2), lambda ci, bi: (ci, 0)),
            _resident((s, s), lambda ci, bi: (0, 0)),
            _resident((s, s), lambda ci, bi: (0, 0)),
        ],
        out_specs=pl.BlockSpec((nb, s, tc), lambda ci, bi: (bi, 0, ci)),
        out_shape=jax.ShapeDtypeStruct((b, s, width), BF16),
        scratch_shapes=[pltpu.VMEM((m, s), BF16) for _ in range(5)],
        compiler_params=_cparams("arbitrary", "arbitrary"),
        name="hyena",
    )(z_hy, z_hy, z_hy, conv_w, conv_w, conv_w, conv_b, conv_b, conv_b,
      pmat, qmat, nyq, pmat, qmat, nyq, bias_t, cmat, smat)


def _rope_tables(s):
    pos = jnp.arange(s, dtype=F32)
    inv = ROPE_THETA ** (-jnp.arange(0, HEAD_DIM, 2, dtype=F32) / HEAD_DIM)
    ang = pos[:, None] * inv[None, :]
    cos, sin = jnp.cos(ang), jnp.sin(ang)
    reps = LANES // HEAD_DIM
    return (jnp.tile(jnp.concatenate([cos, cos], axis=1), (1, reps)),
            jnp.tile(jnp.concatenate([-sin, sin], axis=1), (1, reps)))


def _head_mean_matrix(width):
    i = np.arange(width)
    return jnp.asarray((i[:, None] // HEAD_DIM == i[None, :] // HEAD_DIM).astype(np.float32) / HEAD_DIM)


def _norm_rope(z, g, hm, cos, sin):
    width = z.shape[1]
    hi, lo = _split_bf16(z * z)
    ms = _dot(hi, hm) + _dot(lo, hm)
    zn = z * lax.rsqrt(ms + EPS) * g
    lane = lax.broadcasted_iota(jnp.int32, (1, width), 1)
    half = HEAD_DIM // 2
    partner = jnp.where(lane % HEAD_DIM < half, pltpu.roll(zn, width - half, 1), pltpu.roll(zn, half, 1))
    return zn * cos + partner * sin


_ATT_PREP_ROWS = 256


def _attn_kernel(sink_ref, q_ref, k_ref, v_ref, qg_ref, kg_ref, hmq_ref, hmk_ref, cos_ref, sin_ref,
                 o_ref, qr_ref, kp_ref, vp_ref):
    s, wq = q_ref.shape
    blk = ATT_BLOCK
    span = 3 * blk
    reps = wq // LANES
    lane = lax.broadcasted_iota(jnp.int32, (1, LANES), 1)
    low = lane < HEAD_DIM

    def prep(c, carry):
        rows = pl.ds(pl.multiple_of(c * _ATT_PREP_ROWS, _ATT_PREP_ROWS), _ATT_PREP_ROWS)
        cos = cos_ref[rows, :]
        sin = sin_ref[rows, :]
        q = _norm_rope(q_ref[rows, :].astype(F32), qg_ref[...], hmq_ref[...],
                       jnp.concatenate([cos] * reps, axis=1), jnp.concatenate([sin] * reps, axis=1))
        qr_ref[rows, :] = (q * (HEAD_DIM ** -0.5)).astype(BF16)
        k = _norm_rope(k_ref[rows, :].astype(F32), kg_ref[...], hmk_ref[...], cos, sin)
        v = v_ref[rows, :].astype(F32)
        for src, dst in ((k, kp_ref), (v, vp_ref)):
            swapped = pltpu.roll(src, HEAD_DIM, 1)
            dst[0, 0, rows, 0:LANES] = jnp.where(low, src, 0.0).astype(BF16)
            dst[0, 1, rows, 0:LANES] = jnp.where(low, 0.0, swapped).astype(BF16)
            dst[1, 0, rows, 0:LANES] = jnp.where(low, swapped, 0.0).astype(BF16)
            dst[1, 1, rows, 0:LANES] = jnp.where(low, 0.0, src).astype(BF16)
        ones_lo = jnp.broadcast_to(jnp.where(low, 1.0, 0.0), (_ATT_PREP_ROWS, LANES)).astype(BF16)
        ones_hi = jnp.broadcast_to(jnp.where(low, 0.0, 1.0), (_ATT_PREP_ROWS, LANES)).astype(BF16)
        for kh in range(N_KV_HEADS):
            vp_ref[kh, 0, rows, LANES:2 * LANES] = ones_lo
            vp_ref[kh, 1, rows, LANES:2 * LANES] = ones_hi
        return carry

    lax.fori_loop(0, s // _ATT_PREP_ROWS, prep, 0)

    def block(i, carry):
        q0 = pl.multiple_of(i * blk, blk)
        k0 = pl.multiple_of(jnp.clip(q0 - blk, 0, s - span), blk)
        qrows = pl.ds(q0, blk)
        krows = pl.ds(k0, span)
        rel = (lax.broadcasted_iota(jnp.int32, (blk, span), 1) + (k0 - q0)
               - lax.broadcasted_iota(jnp.int32, (blk, span), 0))
        valid = jnp.abs(rel) <= WINDOW
        valid = jnp.concatenate([valid, valid], axis=0)
        for kh in range(N_KV_HEADS):
            c0 = kh * 2 * LANES
            qs = jnp.concatenate([qr_ref[qrows, c0:c0 + LANES], qr_ref[qrows, c0 + LANES:c0 + 2 * LANES]],
                                 axis=0)
            es, sinks = [], []
            for r in range(2):
                sc = _dot_nt(qs, kp_ref[kh, r, krows, :])
                sc = jnp.where(valid, sc, NEG_INF)
                sink = jnp.concatenate([jnp.full((blk, LANES), sink_ref[4 * kh + r], F32),
                                        jnp.full((blk, LANES), sink_ref[4 * kh + 2 + r], F32)], axis=0)
                m = jnp.maximum(jnp.max(sc, axis=-1, keepdims=True), sink)
                es.append(jnp.exp(sc - jnp.concatenate([m] * (span // LANES), axis=1)).astype(BF16))
                sinks.append(jnp.exp(sink - m))
            od = _dot(es[0], vp_ref[kh, 0, krows, :]) + _dot(es[1], vp_ref[kh, 1, krows, :])
            o = od[:, :LANES] / (od[:, LANES:] + jnp.where(low, sinks[0], sinks[1]))
            o_ref[qrows, c0:c0 + LANES] = o[:blk].astype(BF16)
            o_ref[qrows, c0 + LANES:c0 + 2 * LANES] = o[blk:].astype(BF16)
        return carry

    lax.fori_loop(0, s // blk, block, 0, unroll=2)


def _attention(zq, zk, zv, q_g, k_g, sink, cos_t, sin_t):
    b, s, wq = zq.shape
    wk = zk.shape[2]
    assert wq == N_Q_HEADS * HEAD_DIM and wk == N_KV_HEADS * HEAD_DIM == LANES
    full = lambda shape: pl.BlockSpec(shape, lambda bi: (0,) * len(shape))
    per_b = lambda w: pl.BlockSpec((None, s, w), lambda bi: (bi, 0, 0))
    return pl.pallas_call(
        _attn_kernel,
        grid=(b,),
        in_specs=[
            pl.BlockSpec(memory_space=pltpu.SMEM),
            per_b(wq), per_b(wk), per_b(wk),
            full((1, wq)), full((1, wk)), full((wq, wq)), full((wk, wk)),
            full((s, LANES)), full((s, LANES)),
        ],
        out_specs=per_b(wq),
        out_shape=jax.ShapeDtypeStruct((b, s, wq), BF16),
        scratch_shapes=[pltpu.VMEM((s, wq), BF16),
                        pltpu.VMEM((N_KV_HEADS, 2, s, LANES), BF16),
                        pltpu.VMEM((N_KV_HEADS, 2, s, 2 * LANES), BF16)],
        compiler_params=_cparams("parallel"),
        name="attn",
    )(sink, zq, zk, zv, jnp.tile(q_g, wq // HEAD_DIM).reshape(1, wq),
      jnp.tile(k_g, wk // HEAD_DIM).reshape(1, wk),
      _head_mean_matrix(wq).astype(BF16), _head_mean_matrix(wk).astype(BF16), cos_t, sin_t)


def _gelu(x):
    return 0.5 * x * (1.0 + lax.erf(x * (1.0 / math.sqrt(2.0))))


def _gmlp_kernel(zu_ref, zv_ref, g_ref, b_ref, ws_ref, pb_ref, o_ref):
    tm, width = o_ref.shape
    gch = width // GM_GROUPS
    u = _gelu(zu_ref[...].astype(F32))
    v = _gelu(zv_ref[...].astype(F32))
    mu = jnp.mean(v, axis=-1, keepdims=True)
    var = jnp.mean(jnp.square(v - mu), axis=-1, keepdims=True)
    v = ((v - mu) * lax.rsqrt(var + EPS) * g_ref[...] + b_ref[...]).astype(BF16)
    for c in range(tm // GM_CHUNK):
        r0 = c * GM_CHUNK
        cols = []
        for g in range(GM_GROUPS):
            sv = _dot(ws_ref[g], v[r0:r0 + GM_CHUNK, g * gch:(g + 1) * gch]) + pb_ref[:, g:g + 1]
            cols.append(sv)
        sv = jnp.concatenate(cols, axis=1)
        o_ref[r0:r0 + GM_CHUNK, :] = (u[r0:r0 + GM_CHUNK, :] * sv).astype(BF16)


def _gmlp(zu, zv, ln_g, ln_b, ws_bf16, pos_bias_t, tm):
    b, s, width = zu.shape
    row = lambda bi, si: (bi, si, 0)
    full = lambda shape: pl.BlockSpec(shape, lambda bi, si: (0,) * len(shape))
    return pl.pallas_call(
        _gmlp_kernel,
        grid=(b, s // tm),
        in_specs=[
            pl.BlockSpec((None, tm, width), row), pl.BlockSpec((None, tm, width), row),
            full((1, width)), full((1, width)),
            full((GM_GROUPS, GM_CHUNK, GM_CHUNK)), full((GM_CHUNK, GM_GROUPS)),
        ],
        out_specs=pl.BlockSpec((None, tm, width), row),
        out_shape=jax.ShapeDtypeStruct((b, s, width), BF16),
        compiler_params=_cparams("parallel", "parallel"),
        name="gmlp",
    )(zu, zv, ln_g.reshape(1, width), ln_b.reshape(1, width), ws_bf16, pos_bias_t)


_MERGE_ROWS = 256


def _merge_kernel(x_ref, yh_ref, ya_ref, yg_ref, gate_ref, gt_ref, wb_ref, wo_ref,
                  g2_ref, sc2_ref, sh2_ref, wrh_ref, wrl_ref, x1_ref, h2_ref, aff_ref):
    tm, d = x_ref.shape
    n_e = aff_ref.shape[0]
    for r0 in range(0, tm, _MERGE_ROWS):
        rows = slice(r0, r0 + _MERGE_ROWS)
        acc = None
        for j, y_ref in enumerate((yh_ref, ya_ref, yg_ref)):
            br = _dot(y_ref[rows, :], wb_ref[j])
            term = gate_ref[rows, j * d:(j + 1) * d].astype(F32) * br
            acc = term if acc is None else acc + term
        mix = _dot(acc.astype(BF16), wo_ref[...])
        x1 = x_ref[rows, :] + gt_ref[...] * mix
        x1_ref[rows, :] = x1
        h2 = _rms_mod(x1, g2_ref[...], sc2_ref[...], sh2_ref[...])
        h2_ref[rows, :] = h2.astype(BF16)
        hi, lo = _split_bf16(h2)
        logits = _dot(hi, wrh_ref[...]) + _dot(lo, wrh_ref[...]) + _dot(hi, wrl_ref[...])
        logits = logits.T[:n_e]
        m = jnp.max(logits, axis=0, keepdims=True)
        e = jnp.exp(logits - m)
        aff_ref[:, rows] = e / jnp.sum(e, axis=0, keepdims=True)


def _merge(x, y_hy, y_at, y_gm, gates, gt1, wb_bf16, wo_bf16, g2, sc2, sh2, w_router, tm):
    b, s, d = x.shape
    width = y_hy.shape[2]
    n_e = w_router.shape[1]
    wr = jnp.zeros((d, LANES), F32).at[:, :n_e].set(w_router)
    wr_hi = wr.astype(BF16)
    wr_lo = (wr - wr_hi.astype(F32)).astype(BF16)
    row = lambda bi, si: (bi, si, 0)
    vec = lambda bi, si: (bi, 0, 0)
    full = lambda shape: pl.BlockSpec(shape, lambda bi, si: (0,) * len(shape))
    return pl.pallas_call(
        _merge_kernel,
        grid=(b, s // tm),
        in_specs=[
            pl.BlockSpec((None, tm, d), row),
            pl.BlockSpec((None, tm, width), row), pl.BlockSpec((None, tm, width), row),
            pl.BlockSpec((None, tm, width), row),
            pl.BlockSpec((None, tm, 3 * d), row),
            pl.BlockSpec((None, 1, d), vec),
            _resident((3, width, d), lambda bi, si: (0, 0, 0)),
            _resident((d, d), lambda bi, si: (0, 0)),
            full((1, d)),
            pl.BlockSpec((None, 1, d), vec), pl.BlockSpec((None, 1, d), vec),
            full((d, LANES)), full((d, LANES)),
        ],
        out_specs=[
            pl.BlockSpec((None, tm, d), row),
            pl.BlockSpec((None, tm, d), row),
            pl.BlockSpec((None, n_e, tm), lambda bi, si: (bi, 0, si)),
        ],
        out_shape=[
            jax.ShapeDtypeStruct((b, s, d), F32),
            jax.ShapeDtypeStruct((b, s, d), BF16),
            jax.ShapeDtypeStruct((b, n_e, s), F32),
        ],
        compiler_params=_cparams("parallel", "parallel"),
        name="merge",
    )(x, y_hy, y_at, y_gm, gates, gt1, wb_bf16, wo_bf16, g2, sc2, sh2, wr_hi, wr_lo)


def _prefix_count(x):
    n = x.shape[-1]
    lane = lax.broadcasted_iota(jnp.int32, x.shape, x.ndim - 1)
    shift = 1
    while shift < n:
        x = x + jnp.where(lane >= shift, pltpu.roll(x, shift, x.ndim - 1), 0.0)
        shift *= 2
    return x


_REFINE_STEPS = 30


def _select_kernel(aff_ref, slot_ref, slot_t_ref, *, cap):
    aff = aff_ref[...]
    n_e = aff.shape[0]

    def count_ge(t):
        return jnp.sum(jnp.where(aff >= t, 1.0, 0.0), axis=-1, keepdims=True)

    def coarse(j, cur):
        cand = cur | (jnp.int32(1) << (30 - j))
        return jnp.where(count_ge(pltpu.bitcast(cand, F32)) >= cap, cand, cur)

    bits = lax.fori_loop(0, 31, coarse, jnp.zeros((n_e, 1), jnp.int32))
    lo = pltpu.bitcast(bits, F32)
    hi = pltpu.bitcast(bits + 1, F32)

    def refine(j, lohi):
        lo, hi = lohi
        mid = lo + 0.5 * (hi - lo)
        ok = count_ge(mid) >= cap
        return jnp.where(ok, mid, lo), jnp.where(ok, hi, mid)

    lo, hi = lax.fori_loop(0, _REFINE_STEPS, refine, (lo, hi))
    thr = jnp.min(jnp.where(aff >= lo, aff, jnp.inf), axis=-1, keepdims=True)

    above = jnp.where(aff > thr, 1.0, 0.0)
    tied = jnp.where(aff == thr, 1.0, 0.0)
    need = cap - jnp.sum(above, axis=-1, keepdims=True)
    tie_rank = _prefix_count(tied) - tied
    chosen = above + tied * jnp.where(tie_rank < need, 1.0, 0.0)
    slot = _prefix_count(chosen) - chosen
    slot = jnp.where(chosen > 0.0, slot, -1.0).astype(jnp.int32)
    slot_ref[...] = slot
    pad = jnp.full((LANES - n_e, slot.shape[1]), -1, jnp.int32)
    slot_t_ref[...] = jnp.concatenate([slot, pad], axis=0).T


def _select(aff, cap):
    b, n_e, s = aff.shape
    return pl.pallas_call(
        functools.partial(_select_kernel, cap=cap),
        grid=(b,),
        in_specs=[pl.BlockSpec((None, n_e, s), lambda bi: (bi, 0, 0))],
        out_specs=[
            pl.BlockSpec((None, n_e, s), lambda bi: (bi, 0, 0)),
            pl.BlockSpec((None, s, LANES), lambda bi: (bi, 0, 0)),
        ],
        out_shape=[
            jax.ShapeDtypeStruct((b, n_e, s), jnp.int32),
            jax.ShapeDtypeStruct((b, s, LANES), jnp.int32),
        ],
        compiler_params=_cparams("parallel"),
        name="select",
    )(aff)


def _gather_kernel(slot_ref, aff_ref, h_ref, xg_ref, gate_ref):
    cap = xg_ref.shape[0]
    s = h_ref.shape[0]
    hit = slot_ref[...] == lax.broadcasted_iota(jnp.int32, (cap, s), 0)
    onehot = jnp.where(hit, 1.0, 0.0).astype(BF16)
    xg_ref[...] = _dot(onehot, h_ref[...]).astype(BF16)
    gate_ref[...] = jnp.sum(jnp.where(hit, aff_ref[...], 0.0), axis=-1, keepdims=True)


def _gather(slot, aff, h2, cap):
    b, n_e, s = slot.shape
    d = h2.shape[2]
    slot4 = slot.reshape(b, n_e, 1, s)
    aff4 = aff.reshape(b, n_e, 1, s)
    return pl.pallas_call(
        _gather_kernel,
        grid=(b, n_e),
        in_specs=[
            pl.BlockSpec((None, None, 1, s), lambda bi, e: (bi, e, 0, 0)),
            pl.BlockSpec((None, None, 1, s), lambda bi, e: (bi, e, 0, 0)),
            pl.BlockSpec((None, s, d), lambda bi, e: (bi, 0, 0)),
        ],
        out_specs=[
            pl.BlockSpec((None, None, cap, d), lambda bi, e: (bi, e, 0, 0)),
            pl.BlockSpec((None, None, cap, 1), lambda bi, e: (bi, e, 0, 0)),
        ],
        out_shape=[
            jax.ShapeDtypeStruct((b, n_e, cap, d), BF16),
            jax.ShapeDtypeStruct((b, n_e, cap, 1), F32),
        ],
        compiler_params=_cparams("parallel", "arbitrary"),
        name="gather",
    )(slot4, aff4, h2)


def _expert_kernel(xg_ref, gate_ref, wg_ref, wu_ref, wd_ref, y_ref, *, f_chunk):
    nb, cap, d = xg_ref.shape
    x = xg_ref[...].reshape(nb * cap, d)
    f = wg_ref.shape[1]
    acc = None
    for c in range(f // f_chunk):
        cs = slice(c * f_chunk, (c + 1) * f_chunk)
        a = _dot(x, wg_ref[:, cs])
        u = _dot(x, wu_ref[:, cs])
        act = (a * jax.nn.sigmoid(a) * u).astype(BF16)
        part = _dot(act, wd_ref[cs, :])
        acc = part if acc is None else acc + part
    y = acc * gate_ref[...].reshape(nb * cap, 1)
    y_ref[...] = y.reshape(nb, cap, d).astype(BF16)


def _experts(xg, gate, wg, wu, wd, layer, nb, f_chunk):
    b, n_e, cap, d = xg.shape
    f = wg.shape[3]
    return pl.pallas_call(
        functools.partial(_expert_kernel, f_chunk=f_chunk),
        grid=(n_e, b // nb),
        in_specs=[
            pl.BlockSpec((nb, None, cap, d), lambda e, bg: (bg, e, 0, 0)),
            pl.BlockSpec((nb, None, cap, 1), lambda e, bg: (bg, e, 0, 0)),
            pl.BlockSpec((None, None, d, f), lambda e, bg: (layer, e, 0, 0)),
            pl.BlockSpec((None, None, d, f), lambda e, bg: (layer, e, 0, 0)),
            pl.BlockSpec((None, None, f, d), lambda e, bg: (layer, e, 0, 0)),
        ],
        out_specs=pl.BlockSpec((nb, None, cap, d), lambda e, bg: (bg, e, 0, 0)),
        out_shape=jax.ShapeDtypeStruct((b, n_e, cap, d), BF16),
        compiler_params=_cparams("arbitrary", "arbitrary"),
        name="expert",
    )(xg, gate, wg, wu, wd)


def _combine_kernel(x_ref, gt_ref, slot_t_ref, y_ref, o_ref):
    tm = x_ref.shape[0]
    n_e, cap, d = y_ref.shape
    slot_t = slot_t_ref[...]
    lane = lax.broadcasted_iota(jnp.int32, (tm, cap), 1)
    acc = jnp.zeros((tm, d), F32)
    for e in range(n_e):
        onehot = jnp.where(slot_t[:, e:e + 1] == lane, 1.0, 0.0).astype(BF16)
        acc = acc + _dot(onehot, y_ref[e])
    o_ref[...] = x_ref[...] + gt_ref[...] * acc


def _combine(x1, gt2, slot_t, y, tm):
    b, s, d = x1.shape
    _, n_e, cap, _ = y.shape
    return pl.pallas_call(
        _combine_kernel,
        grid=(b, s // tm),
        in_specs=[
            pl.BlockSpec((None, tm, d), lambda bi, si: (bi, si, 0)),
            pl.BlockSpec((None, 1, d), lambda bi, si: (bi, 0, 0)),
            pl.BlockSpec((None, tm, LANES), lambda bi, si: (bi, si, 0)),
            pl.BlockSpec((None, n_e, cap, d), lambda bi, si: (bi, 0, 0, 0)),
        ],
        out_specs=pl.BlockSpec((None, tm, d), lambda bi, si: (bi, si, 0)),
        out_shape=jax.ShapeDtypeStruct((b, s, d), F32),
        compiler_params=_cparams("parallel", "arbitrary"),
        name="combine",
    )(x1, gt2, slot_t, y)


def kernel(x, c, w_mod, b_mod, norm1_g, norm2_g, w_in, hy_conv_w, hy_conv_b, hy_f1_w, hy_f1_b, hy_f1_freq,
           hy_f2_w, hy_f2_b, hy_f2_freq, hy_f3_w, hy_bias, q_norm_g, k_norm_g, attn_sink, gm_ln_g, gm_ln_b,
           gm_ws, gm_b, w_branch, w_out, w_router, w_e_gate, w_e_up, w_e_down):
    b, s, d = x.shape
    depth = w_mod.shape[0]
    width = hy_bias.shape[2]
    cap = EC_CAPACITY * s // N_EXPERTS
    assert w_in.shape[2] == IN_COLS and s % 512 == 0 and b % 2 == 0
    tm = 512

    cmat_np, smat_np = _dft_tables(s)
    cmat = jnp.asarray(cmat_np).astype(BF16)
    smat = jnp.asarray(smat_np).astype(BF16)
    cos_t, sin_t = _rope_tables(s)
    mod = _modulation(c, w_mod, b_mod)
    wg_all, wu_all, wd_all = (w.astype(BF16) for w in (w_e_gate, w_e_up, w_e_down))

    for l in range(depth):
        sh1, sc1, gt1, sh2, sc2, gt2 = (mod[l, j] for j in range(6))
        z_hy, z_q, z_k, z_v, z_gu, z_gv, gates = _inproj(
            x, sc1, sh1, norm1_g[l].reshape(1, d), w_in[l].astype(BF16), tm)
        pmat, qmat, nyq = _hyena_filters(s, width, hy_f1_w[l], hy_f1_b[l], hy_f1_freq[l], hy_f2_w[l],
                                         hy_f2_b[l], hy_f2_freq[l], hy_f3_w[l], cmat, smat)
        y_hy = _hyena(z_hy, hy_conv_w[l], hy_conv_b[l], pmat, qmat, nyq, hy_bias[l].T, cmat, smat)
        y_at = _attention(z_q, z_k, z_v, q_norm_g[l], k_norm_g[l], attn_sink[l], cos_t, sin_t)
        y_gm = _gmlp(z_gu, z_gv, gm_ln_g[l], gm_ln_b[l], gm_ws[l].astype(BF16), gm_b[l].T, tm)
        x1, h2, aff = _merge(x, y_hy, y_at, y_gm, gates, gt1, w_branch[l].astype(BF16),
                             w_out[l].astype(BF16), norm2_g[l].reshape(1, d), sc2, sh2, w_router[l], tm)
        slot, slot_t = _select(aff, cap)
        xg, gate = _gather(slot, aff, h2, cap)
        y = _experts(xg, gate, wg_all, wu_all, wd_all, l, nb=2, f_chunk=1024)
        x = _combine(x1, gt2, slot_t, y, tm)
    return x
```

```python
import functools
import math

import numpy as np
import jax
import jax.numpy as jnp
from jax import lax
from jax.experimental import pallas as pl
from jax.experimental.pallas import tpu as pltpu

F32 = jnp.float32
BF16 = jnp.bfloat16
HIGHEST = lax.Precision.HIGHEST

HEAD_DIM = 64
N_Q_HEADS = 8
N_KV_HEADS = 2
WINDOW = 128
ATT_BLOCK = 128
ROPE_THETA = 10000.0
HY_POS_BANDS = 8
HY_DECAY_TARGET = 1e-2
HY_FAST_DECAY = 0.3
HY_SLOW_DECAY = 1.5
HY_MOD_SHIFT = 0.05
GM_GROUPS = 4
GM_CHUNK = 128
N_EXPERTS = 16
EC_CAPACITY = 2
EPS = 1e-6
NEG_INF = -1e30

V7X_VMEM_LIMIT_BYTES = 56 * 1024 * 1024
LANES = 128


def _cparams(*sem):
    return pltpu.CompilerParams(dimension_semantics=sem, vmem_limit_bytes=V7X_VMEM_LIMIT_BYTES)


def _resident(shape, index_map):
    return pl.BlockSpec(shape, index_map, pipeline_mode=pl.Buffered(1))


def _dot(a, b, precision=None):
    return jnp.dot(a, b, preferred_element_type=F32, precision=precision)


def _dot_nt(a, b):
    return lax.dot_general(a, b, (((1,), (1,)), ((), ())), preferred_element_type=F32)


def _split_bf16(x):
    hi = x.astype(BF16)
    return hi, (x - hi.astype(F32)).astype(BF16)


def _mod_kernel(c_ref, w_ref, b_ref, o_ref):
    c = c_ref[...]
    cond = c * jax.nn.sigmoid(c)
    o_ref[...] = _dot(cond, w_ref[...], HIGHEST) + b_ref[...]


def _modulation(c, w_mod, b_mod):
    depth, d, _ = w_mod.shape
    b = c.shape[0]
    out = pl.pallas_call(
        _mod_kernel,
        grid=(depth, 6),
        in_specs=[
            pl.BlockSpec((b, d), lambda l, j: (0, 0)),
            pl.BlockSpec((None, d, d), lambda l, j: (l, 0, j)),
            pl.BlockSpec((None, None, 1, d), lambda l, j: (l, j, 0, 0)),
        ],
        out_specs=pl.BlockSpec((None, None, b, d), lambda l, j: (l, j, 0, 0)),
        out_shape=jax.ShapeDtypeStruct((depth, 6, b, d), F32),
        compiler_params=_cparams("arbitrary", "arbitrary"),
        name="mod",
    )(c, w_mod, b_mod.reshape(depth, 6, 1, d))
    return out.reshape(depth, 6, b, 1, d)


_W_HY, _W_Q, _W_KV, _W_GM, _W_GATE = 1536, 512, 128, 512, 3072
_C_Q = _W_HY
_C_K = _C_Q + _W_Q
_C_V = _C_K + _W_KV
_C_GU = _C_V + _W_KV
_C_GV = _C_GU + _W_GM
_C_GATE = _C_GV + _W_GM
IN_COLS = _C_GATE + _W_GATE


def _rms_mod(x, g, sc, sh):
    ms = jnp.mean(x * x, axis=-1, keepdims=True)
    return (x * lax.rsqrt(ms + EPS) * g) * (1.0 + sc) + sh


def _inproj_kernel(x_ref, sc_ref, sh_ref, g_ref, w_ref,
                   hy_ref, q_ref, k_ref, v_ref, gu_ref, gv_ref, gate_ref):
    hb = _rms_mod(x_ref[...], g_ref[...], sc_ref[...], sh_ref[...]).astype(BF16)

    def proj(c0, width):
        return _dot(hb, w_ref[:, c0:c0 + width])

    for j in range(3):
        hy_ref[:, j * 512:(j + 1) * 512] = proj(j * 512, 512).astype(BF16)
    q_ref[...] = proj(_C_Q, _W_Q).astype(BF16)
    k_ref[...] = proj(_C_K, _W_KV).astype(BF16)
    v_ref[...] = proj(_C_V, _W_KV).astype(BF16)
    gu_ref[...] = proj(_C_GU, _W_GM).astype(BF16)
    gv_ref[...] = proj(_C_GV, _W_GM).astype(BF16)
    for j in range(3):
        z = proj(_C_GATE + j * 1024, 1024)
        gate_ref[:, j * 1024:(j + 1) * 1024] = jax.nn.sigmoid(z).astype(BF16)


def _inproj(x, sc, sh, g, w_in_bf16, tm):
    b, s, d = x.shape
    widths = (_W_HY, _W_Q, _W_KV, _W_KV, _W_GM, _W_GM, _W_GATE)
    row = lambda bi, si: (bi, si, 0)
    vec = lambda bi, si: (bi, 0, 0)
    return pl.pallas_call(
        _inproj_kernel,
        grid=(b, s // tm),
        in_specs=[
            pl.BlockSpec((None, tm, d), row),
            pl.BlockSpec((None, 1, d), vec),
            pl.BlockSpec((None, 1, d), vec),
            pl.BlockSpec((1, d), lambda bi, si: (0, 0)),
            _resident((d, IN_COLS), lambda bi, si: (0, 0)),
        ],
        out_specs=[pl.BlockSpec((None, tm, w), row) for w in widths],
        out_shape=[jax.ShapeDtypeStruct((b, s, w), BF16) for w in widths],
        compiler_params=_cparams("parallel", "parallel"),
        name="inproj",
    )(x, sc, sh, g, w_in_bf16)


def _dft_tables(s):
    n = 2 * s
    idx = (np.arange(s, dtype=np.int64)[:, None] * np.arange(s, dtype=np.int64)[None, :]) % n
    ang = idx.astype(np.float64) * (2.0 * np.pi / n)
    return np.cos(ang).astype(np.float32), np.sin(ang).astype(np.float32)


def _filter_tables(s, width):
    f32 = np.float32
    pos = np.arange(s, dtype=f32)
    t = np.linspace(0.0, 1.0, s, dtype=f32)
    w = (f32(2.0 * math.pi) * pos / f32(s)).astype(f32)
    bands = np.linspace(1e-4, HY_POS_BANDS - 1, HY_POS_BANDS, dtype=f32)
    ang = w[:, None] * bands[None, :]
    feats = np.concatenate([t[:, None], np.cos(ang), -np.sin(ang)], axis=-1).astype(f32)
    deltas = np.abs(np.linspace(math.log(HY_DECAY_TARGET) / HY_FAST_DECAY,
                                math.log(HY_DECAY_TARGET) / HY_SLOW_DECAY, width, dtype=f32))
    window = (np.exp(-t[:, None] * deltas[None, :]) + f32(HY_MOD_SHIFT)).astype(f32)
    mirror = (s - np.arange(s)) % s
    feats_pad = np.zeros((s, LANES), f32)
    feats_pad[:, :feats.shape[1]] = feats
    window_rev = window[mirror].copy()
    window_rev[0] = 0.0
    return (np.ascontiguousarray(feats_pad.T), np.ascontiguousarray(feats_pad[mirror].T),
            np.ascontiguousarray(window.T), np.ascontiguousarray(window_rev.T))


def _alt_sign(s):
    lane = lax.broadcasted_iota(jnp.int32, (1, s), 1)
    return jnp.where(lane % 2 == 0, 1.0, -1.0).astype(F32)


def _filter_kernel(ff_ref, fr_ref, wf_ref, wr_ref, f1w_ref, f1b_ref, f1f_ref, f2w_ref, f2b_ref, f2f_ref,
                   f3f_ref, f3b_ref, c_ref, s_ref, p_ref, q_ref, kn_ref, af_ref, ar_ref):
    tr, s = p_ref.shape
    n = 2 * s

    @pl.when(pl.program_id(0) == 0)
    def _():
        def mlp(feats):
            a = jnp.sin(f1f_ref[...] * (_dot(f1w_ref[...], feats, HIGHEST) + f1b_ref[...]))
            return jnp.sin(f2f_ref[...] * (_dot(f2w_ref[...], a, HIGHEST) + f2b_ref[...]))
        af_ref[...] = mlp(ff_ref[...])
        ar_ref[...] = mlp(fr_ref[...])

    h1 = _dot(f3f_ref[...], af_ref[...], HIGHEST) * wf_ref[...]
    h2 = _dot(f3b_ref[...], ar_ref[...], HIGHEST) * wr_ref[...]
    l1 = jnp.sum(jnp.abs(h1), axis=1, keepdims=True) + jnp.sum(jnp.abs(h2), axis=1, keepdims=True)
    k1 = h1 / l1
    k2 = h2 / l1
    kk = jnp.concatenate([k1, k2], axis=0).astype(BF16)
    ck = _dot(kk, c_ref[...])
    sk = _dot(kk, s_ref[...])
    sign = _alt_sign(s)
    kc = ck[:tr] + sign * ck[tr:]
    ks = sk[:tr] + sign * sk[tr:]
    lane = lax.broadcasted_iota(jnp.int32, (1, s), 1)
    wgt = jnp.where(lane == 0, 1.0 / n, 2.0 / n).astype(F32)
    p_ref[...] = kc * wgt
    q_ref[...] = ks * wgt
    kn_ref[...] = jnp.sum(sign * (k1 + k2), axis=1, keepdims=True) * (1.0 / n)


def _hyena_filters(s, width, f1_w, f1_b, f1_freq, f2_w, f2_b, f2_freq, f3_w, cmat, smat, tr=256):
    feats_f, feats_r, win_f, win_r = (jnp.asarray(a) for a in _filter_tables(s, width))
    hid = f1_w.shape[1]
    f1w_t = jnp.zeros((hid, LANES), F32).at[:, :f1_w.shape[0]].set(f1_w.T)
    f3_t = f3_w.T
    n_row = 2 * width
    nrt = n_row // tr
    wpt = width // tr
    col = lambda v: v.reshape(hid, 1)
    full = lambda shape: pl.BlockSpec(shape, lambda j: (0,) * len(shape))
    return pl.pallas_call(
        _filter_kernel,
        grid=(nrt,),
        in_specs=[
            full((LANES, s)), full((LANES, s)),
            pl.BlockSpec((tr, s), lambda j: (j % wpt, 0)),
            pl.BlockSpec((tr, s), lambda j: (j % wpt, 0)),
            full((hid, LANES)), full((hid, 1)), full((hid, 1)),
            full((hid, hid)), full((hid, 1)), full((hid, 1)),
            pl.BlockSpec((tr, hid), lambda j: (j, 0)),
            pl.BlockSpec((tr, hid), lambda j: (j + nrt, 0)),
            _resident((s, s), lambda j: (0, 0)),
            _resident((s, s), lambda j: (0, 0)),
        ],
        out_specs=[
            pl.BlockSpec((tr, s), lambda j: (j, 0)),
            pl.BlockSpec((tr, s), lambda j: (j, 0)),
            pl.BlockSpec((tr, 1), lambda j: (j, 0)),
        ],
        out_shape=[
            jax.ShapeDtypeStruct((n_row, s), F32),
            jax.ShapeDtypeStruct((n_row, s), F32),
            jax.ShapeDtypeStruct((n_row, 1), F32),
        ],
        scratch_shapes=[pltpu.VMEM((hid, s), F32), pltpu.VMEM((hid, s), F32)],
        compiler_params=_cparams("arbitrary"),
        name="hyena_filter",
    )(feats_f, feats_r, win_f, win_r, f1w_t, col(f1_b), col(f1_freq),
      f2_w.T, col(f2_b), col(f2_freq), f3_t, f3_t, cmat, smat)


_HY_CHUNK = 512


def _hyena_kernel(zv_ref, z1_ref, z2_ref, wv_ref, w1_ref, w2_ref, bv_ref, b1_ref, b2_ref,
                  p0_ref, q0_ref, n0_ref, p1_ref, q1_ref, n1_ref, bias_ref, c_ref, s_ref, o_ref,
                  u_ref, x1_ref, x2_ref, yr_ref, yi_ref):
    nb, s, tc = o_ref.shape
    row = lax.broadcasted_iota(jnp.int32, (s, 1), 0)
    sign = _alt_sign(s)
    chunks = [slice(c0, c0 + _HY_CHUNK) for c0 in range(0, s, _HY_CHUNK)]

    def short_conv_t(z_ref, w_ref, b_ref, dst_ref):
        for i in range(nb):
            z = z_ref[i].astype(F32)
            zp = jnp.where(row == 0, 0.0, pltpu.roll(z, 1, 0))
            zn = jnp.where(row == s - 1, 0.0, pltpu.roll(z, s - 1, 0))
            y = zp * w_ref[0:1, :] + z * w_ref[1:2, :] + zn * w_ref[2:3, :] + b_ref[...]
            dst_ref[i * tc:(i + 1) * tc, :] = y.T.astype(BF16)

    def per_batch(t):
        return jnp.concatenate([t] * nb, axis=0)

    def long_conv(p_ref, q_ref, n_ref, bias, x_ref, emit):
        nyq = jnp.sum(sign * u_ref[...].astype(F32), axis=1, keepdims=True) * per_batch(n_ref[...])
        for ch in chunks:
            a = _dot(u_ref[...], c_ref[:, ch])
            b = _dot(u_ref[...], s_ref[:, ch])
            p = per_batch(p_ref[:, ch])
            q = per_batch(q_ref[:, ch])
            yr_ref[:, ch] = (a * p - b * q).astype(BF16)
            yi_ref[:, ch] = (a * q + b * p).astype(BF16)
        for ch in chunks:
            y = _dot(yr_ref[...], c_ref[:, ch]) + _dot(yi_ref[...], s_ref[:, ch])
            y = y + sign[:, ch] * nyq + u_ref[:, ch].astype(F32) * bias
            emit(ch, x_ref[:, ch].astype(F32) * y)

    short_conv_t(zv_ref, wv_ref, bv_ref, u_ref)
    short_conv_t(z1_ref, w1_ref, b1_ref, x1_ref)
    short_conv_t(z2_ref, w2_ref, b2_ref, x2_ref)

    def to_u(ch, y):
        u_ref[:, ch] = y.astype(BF16)

    def to_out(ch, y):
        for i in range(nb):
            o_ref[i, ch, :] = y[i * tc:(i + 1) * tc].T.astype(BF16)

    long_conv(p0_ref, q0_ref, n0_ref, per_batch(bias_ref[:, 0:1]), x1_ref, to_u)
    long_conv(p1_ref, q1_ref, n1_ref, per_batch(bias_ref[:, 1:2]), x2_ref, to_out)


def _hyena(z_hy, conv_w, conv_b, pmat, qmat, nyq, bias_t, cmat, smat, tc=256, nb=2):
    b, s, _ = z_hy.shape
    width = bias_t.shape[0]
    nct = width // tc
    conv_b = conv_b.reshape(1, 3 * width)
    zcol = lambda k: pl.BlockSpec((nb, s, tc), lambda ci, bi, k=k: (bi, 0, ci + k * nct),
                                  pipeline_mode=pl.Buffered(1))
    wcol = lambda rows, k: pl.BlockSpec((rows, tc), lambda ci, bi, k=k: (0, ci + k * nct))
    spec = lambda k: _resident((tc, s), lambda ci, bi, k=k: (ci + k * nct, 0))
    nspec = lambda k: pl.BlockSpec((tc, 1), lambda ci, bi, k=k: (ci + k * nct, 0))
    m = nb * tc
    return pl.pallas_call(
        _hyena_kernel,
        grid=(nct, b // nb),
        in_specs=[
            zcol(0), zcol(1), zcol(2),
            wcol(3, 0), wcol(3, 1), wcol(3, 2),
            wcol(1, 0), wcol(1, 1), wcol(1, 2),
            spec(0), spec(0), nspec(0),
            spec(1), spec(1), nspec(1),
            pl.BlockSpec((tc, 2), lambda ci, bi: (ci, 0)),
            _resident((s, s), lambda ci, bi: (0, 0)),
            _resident((s, s), lambda ci, bi: (0, 0)),
        ],
        out_specs=pl.BlockSpec((nb, s, tc), lambda ci, bi: (bi, 0, ci)),
        out_shape=jax.ShapeDtypeStruct((b, s, width), BF16),
        scratch_shapes=[pltpu.VMEM((m, s), BF16) for _ in range(5)],
        compiler_params=_cparams("arbitrary", "arbitrary"),
        name="hyena",
    )(z_hy, z_hy, z_hy, conv_w, conv_w, conv_w, conv_b, conv_b, conv_b,
      pmat, qmat, nyq, pmat, qmat, nyq, bias_t, cmat, smat)


def _rope_tables(s):
    pos = jnp.arange(s, dtype=F32)
    inv = ROPE_THETA ** (-jnp.arange(0, HEAD_DIM, 2, dtype=F32) / HEAD_DIM)
    ang = pos[:, None] * inv[None, :]
    cos, sin = jnp.cos(ang), jnp.sin(ang)
    reps = LANES // HEAD_DIM
    return (jnp.tile(jnp.concatenate([cos, cos], axis=1), (1, reps)),
            jnp.tile(jnp.concatenate([-sin, sin], axis=1), (1, reps)))


def _head_mean_matrix(width):
    i = np.arange(width)
    return jnp.asarray((i[:, None] // HEAD_DIM == i[None, :] // HEAD_DIM).astype(np.float32) / HEAD_DIM)


def _norm_rope(z, g, hm, cos, sin):
    width = z.shape[1]
    hi, lo = _split_bf16(z * z)
    ms = _dot(hi, hm) + _dot(lo, hm)
    zn = z * lax.rsqrt(ms + EPS) * g
    lane = lax.broadcasted_iota(jnp.int32, (1, width), 1)
    half = HEAD_DIM // 2
    partner = jnp.where(lane % HEAD_DIM < half, pltpu.roll(zn, width - half, 1), pltpu.roll(zn, half, 1))
    return zn * cos + partner * sin


_ATT_PREP_ROWS = 256


def _attn_kernel(sink_ref, q_ref, k_ref, v_ref, qg_ref, kg_ref, hmq_ref, hmk_ref, cos_ref, sin_ref,
                 o_ref, qr_ref, kp_ref, vp_ref):
    s, wq = q_ref.shape
    blk = ATT_BLOCK
    span = 3 * blk
    reps = wq // LANES
    lane = lax.broadcasted_iota(jnp.int32, (1, LANES), 1)
    low = lane < HEAD_DIM

    def prep(c, carry):
        rows = pl.ds(pl.multiple_of(c * _ATT_PREP_ROWS, _ATT_PREP_ROWS), _ATT_PREP_ROWS)
        cos = cos_ref[rows, :]
        sin = sin_ref[rows, :]
        q = _norm_rope(q_ref[rows, :].astype(F32), qg_ref[...], hmq_ref[...],
                       jnp.concatenate([cos] * reps, axis=1), jnp.concatenate([sin] * reps, axis=1))
        qr_ref[rows, :] = (q * (HEAD_DIM ** -0.5)).astype(BF16)
        k = _norm_rope(k_ref[rows, :].astype(F32), kg_ref[...], hmk_ref[...], cos, sin)
        v = v_ref[rows, :].astype(F32)
        for src, dst in ((k, kp_ref), (v, vp_ref)):
            swapped = pltpu.roll(src, HEAD_DIM, 1)
            dst[0, 0, rows, 0:LANES] = jnp.where(low, src, 0.0).astype(BF16)
            dst[0, 1, rows, 0:LANES] = jnp.where(low, 0.0, swapped).astype(BF16)
            dst[1, 0, rows, 0:LANES] = jnp.where(low, swapped, 0.0).astype(BF16)
            dst[1, 1, rows, 0:LANES] = jnp.where(low, 0.0, src).astype(BF16)
        ones_lo = jnp.broadcast_to(jnp.where(low, 1.0, 0.0), (_ATT_PREP_ROWS, LANES)).astype(BF16)
        ones_hi = jnp.broadcast_to(jnp.where(low, 0.0, 1.0), (_ATT_PREP_ROWS, LANES)).astype(BF16)
        for kh in range(N_KV_HEADS):
            vp_ref[kh, 0, rows, LANES:2 * LANES] = ones_lo
            vp_ref[kh, 1, rows, LANES:2 * LANES] = ones_hi
        return carry

    lax.fori_loop(0, s // _ATT_PREP_ROWS, prep, 0)

    def block(i, carry):
        q0 = pl.multiple_of(i * blk, blk)
        k0 = pl.multiple_of(jnp.clip(q0 - blk, 0, s - span), blk)
        qrows = pl.ds(q0, blk)
        krows = pl.ds(k0, span)
        rel = (lax.broadcasted_iota(jnp.int32, (blk, span), 1) + (k0 - q0)
               - lax.broadcasted_iota(jnp.int32, (blk, span), 0))
        valid = jnp.abs(rel) <= WINDOW
        valid = jnp.concatenate([valid, valid], axis=0)
        for kh in range(N_KV_HEADS):
            c0 = kh * 2 * LANES
            qs = jnp.concatenate([qr_ref[qrows, c0:c0 + LANES], qr_ref[qrows, c0 + LANES:c0 + 2 * LANES]],
                                 axis=0)
            es, sinks = [], []
            for r in range(2):
                sc = _dot_nt(qs, kp_ref[kh, r, krows, :])
                sc = jnp.where(valid, sc, NEG_INF)
                sink = jnp.concatenate([jnp.full((blk, LANES), sink_ref[4 * kh + r], F32),
                                        jnp.full((blk, LANES), sink_ref[4 * kh + 2 + r], F32)], axis=0)
                m = jnp.maximum(jnp.max(sc, axis=-1, keepdims=True), sink)
                es.append(jnp.exp(sc - jnp.concatenate([m] * (span // LANES), axis=1)).astype(BF16))
                sinks.append(jnp.exp(sink - m))
            od = _dot(es[0], vp_ref[kh, 0, krows, :]) + _dot(es[1], vp_ref[kh, 1, krows, :])
            o = od[:, :LANES] / (od[:, LANES:] + jnp.where(low, sinks[0], sinks[1]))
            o_ref[qrows, c0:c0 + LANES] = o[:blk].astype(BF16)
            o_ref[qrows, c0 + LANES:c0 + 2 * LANES] = o[blk:].astype(BF16)
        return carry

    lax.fori_loop(0, s // blk, block, 0, unroll=2)


def _attention(zq, zk, zv, q_g, k_g, sink, cos_t, sin_t):
    b, s, wq = zq.shape
    wk = zk.shape[2]
    assert wq == N_Q_HEADS * HEAD_DIM and wk == N_KV_HEADS * HEAD_DIM == LANES
    full = lambda shape: pl.BlockSpec(shape, lambda bi: (0,) * len(shape))
    per_b = lambda w: pl.BlockSpec((None, s, w), lambda bi: (bi, 0, 0))
    return pl.pallas_call(
        _attn_kernel,
        grid=(b,),
        in_specs=[
            pl.BlockSpec(memory_space=pltpu.SMEM),
            per_b(wq), per_b(wk), per_b(wk),
            full((1, wq)), full((1, wk)), full((wq, wq)), full((wk, wk)),
            full((s, LANES)), full((s, LANES)),
        ],
        out_specs=per_b(wq),
        out_shape=jax.ShapeDtypeStruct((b, s, wq), BF16),
        scratch_shapes=[pltpu.VMEM((s, wq), BF16),
                        pltpu.VMEM((N_KV_HEADS, 2, s, LANES), BF16),
                        pltpu.VMEM((N_KV_HEADS, 2, s, 2 * LANES), BF16)],
        compiler_params=_cparams("parallel"),
        name="attn",
    )(sink, zq, zk, zv, jnp.tile(q_g, wq // HEAD_DIM).reshape(1, wq),
      jnp.tile(k_g, wk // HEAD_DIM).reshape(1, wk),
      _head_mean_matrix(wq).astype(BF16), _head_mean_matrix(wk).astype(BF16), cos_t, sin_t)


def _gelu(x):
    return 0.5 * x * (1.0 + lax.erf(x * (1.0 / math.sqrt(2.0))))


def _gmlp_kernel(zu_ref, zv_ref, g_ref, b_ref, ws_ref, pb_ref, o_ref):
    tm, width = o_ref.shape
    gch = width // GM_GROUPS
    u = _gelu(zu_ref[...].astype(F32))
    v = _gelu(zv_ref[...].astype(F32))
    mu = jnp.mean(v, axis=-1, keepdims=True)
    var = jnp.mean(jnp.square(v - mu), axis=-1, keepdims=True)
    v = ((v - mu) * lax.rsqrt(var + EPS) * g_ref[...] + b_ref[...]).astype(BF16)
    for c in range(tm // GM_CHUNK):
        r0 = c * GM_CHUNK
        cols = []
        for g in range(GM_GROUPS):
            sv = _dot(ws_ref[g], v[r0:r0 + GM_CHUNK, g * gch:(g + 1) * gch]) + pb_ref[:, g:g + 1]
            cols.append(sv)
        sv = jnp.concatenate(cols, axis=1)
        o_ref[r0:r0 + GM_CHUNK, :] = (u[r0:r0 + GM_CHUNK, :] * sv).astype(BF16)


def _gmlp(zu, zv, ln_g, ln_b, ws_bf16, pos_bias_t, tm):
    b, s, width = zu.shape
    row = lambda bi, si: (bi, si, 0)
    full = lambda shape: pl.BlockSpec(shape, lambda bi, si: (0,) * len(shape))
    return pl.pallas_call(
        _gmlp_kernel,
        grid=(b, s // tm),
        in_specs=[
            pl.BlockSpec((None, tm, width), row), pl.BlockSpec((None, tm, width), row),
            full((1, width)), full((1, width)),
            full((GM_GROUPS, GM_CHUNK, GM_CHUNK)), full((GM_CHUNK, GM_GROUPS)),
        ],
        out_specs=pl.BlockSpec((None, tm, width), row),
        out_shape=jax.ShapeDtypeStruct((b, s, width), BF16),
        compiler_params=_cparams("parallel", "parallel"),
        name="gmlp",
    )(zu, zv, ln_g.reshape(1, width), ln_b.reshape(1, width), ws_bf16, pos_bias_t)


_MERGE_ROWS = 256


def _merge_kernel(x_ref, yh_ref, ya_ref, yg_ref, gate_ref, gt_ref, wb_ref, wo_ref,
                  g2_ref, sc2_ref, sh2_ref, wrh_ref, wrl_ref, x1_ref, h2_ref, aff_ref):
    tm, d = x_ref.shape
    n_e = aff_ref.shape[0]
    for r0 in range(0, tm, _MERGE_ROWS):
        rows = slice(r0, r0 + _MERGE_ROWS)
        acc = None
        for j, y_ref in enumerate((yh_ref, ya_ref, yg_ref)):
            br = _dot(y_ref[rows, :], wb_ref[j])
            term = gate_ref[rows, j * d:(j + 1) * d].astype(F32) * br
            acc = term if acc is None else acc + term
        mix = _dot(acc.astype(BF16), wo_ref[...])
        x1 = x_ref[rows, :] + gt_ref[...] * mix
        x1_ref[rows, :] = x1
        h2 = _rms_mod(x1, g2_ref[...], sc2_ref[...], sh2_ref[...])
        h2_ref[rows, :] = h2.astype(BF16)
        hi, lo = _split_bf16(h2)
        logits = _dot(hi, wrh_ref[...]) + _dot(lo, wrh_ref[...]) + _dot(hi, wrl_ref[...])
        logits = logits.T[:n_e]
        m = jnp.max(logits, axis=0, keepdims=True)
        e = jnp.exp(logits - m)
        aff_ref[:, rows] = e / jnp.sum(e, axis=0, keepdims=True)


def _merge(x, y_hy, y_at, y_gm, gates, gt1, wb_bf16, wo_bf16, g2, sc2, sh2, w_router, tm):
    b, s, d = x.shape
    width = y_hy.shape[2]
    n_e = w_router.shape[1]
    wr = jnp.zeros((d, LANES), F32).at[:, :n_e].set(w_router)
    wr_hi = wr.astype(BF16)
    wr_lo = (wr - wr_hi.astype(F32)).astype(BF16)
    row = lambda bi, si: (bi, si, 0)
    vec = lambda bi, si: (bi, 0, 0)
    full = lambda shape: pl.BlockSpec(shape, lambda bi, si: (0,) * len(shape))
    return pl.pallas_call(
        _merge_kernel,
        grid=(b, s // tm),
        in_specs=[
            pl.BlockSpec((None, tm, d), row),
            pl.BlockSpec((None, tm, width), row), pl.BlockSpec((None, tm, width), row),
            pl.BlockSpec((None, tm, width), row),
            pl.BlockSpec((None, tm, 3 * d), row),
            pl.BlockSpec((None, 1, d), vec),
            _resident((3, width, d), lambda bi, si: (0, 0, 0)),
            _resident((d, d), lambda bi, si: (0, 0)),
            full((1, d)),
            pl.BlockSpec((None, 1, d), vec), pl.BlockSpec((None, 1, d), vec),
            full((d, LANES)), full((d, LANES)),
        ],
        out_specs=[
            pl.BlockSpec((None, tm, d), row),
            pl.BlockSpec((None, tm, d), row),
            pl.BlockSpec((None, n_e, tm), lambda bi, si: (bi, 0, si)),
        ],
        out_shape=[
            jax.ShapeDtypeStruct((b, s, d), F32),
            jax.ShapeDtypeStruct((b, s, d), BF16),
            jax.ShapeDtypeStruct((b, n_e, s), F32),
        ],
        compiler_params=_cparams("parallel", "parallel"),
        name="merge",
    )(x, y_hy, y_at, y_gm, gates, gt1, wb_bf16, wo_bf16, g2, sc2, sh2, wr_hi, wr_lo)


def _prefix_count(x):
    n = x.shape[-1]
    lane = lax.broadcasted_iota(jnp.int32, x.shape, x.ndim - 1)
    shift = 1
    while shift < n:
        x = x + jnp.where(lane >= shift, pltpu.roll(x, shift, x.ndim - 1), 0.0)
        shift *= 2
    return x


_REFINE_STEPS = 30


def _select_kernel(aff_ref, slot_ref, slot_t_ref, *, cap, n_e):
    aff = aff_ref[...]
    n_rows = aff.shape[0]

    def count_ge(t):
        return jnp.sum(jnp.where(aff >= t, 1.0, 0.0), axis=-1, keepdims=True)

    def coarse(j, cur):
        cand = cur | (jnp.int32(1) << (30 - j))
        return jnp.where(count_ge(pltpu.bitcast(cand, F32)) >= cap, cand, cur)

    bits = lax.fori_loop(0, 31, coarse, jnp.zeros((n_rows, 1), jnp.int32))
    lo = pltpu.bitcast(bits, F32)
    hi = pltpu.bitcast(bits + 1, F32)

    def refine(j, lohi):
        lo, hi = lohi
        mid = lo + 0.5 * (hi - lo)
        ok = count_ge(mid) >= cap
        return jnp.where(ok, mid, lo), jnp.where(ok, hi, mid)

    lo, hi = lax.fori_loop(0, _REFINE_STEPS, refine, (lo, hi))
    thr = jnp.min(jnp.where(aff >= lo, aff, jnp.inf), axis=-1, keepdims=True)

    above = jnp.where(aff > thr, 1.0, 0.0)
    tied = jnp.where(aff == thr, 1.0, 0.0)
    need = cap - jnp.sum(above, axis=-1, keepdims=True)
    tie_rank = _prefix_count(tied) - tied
    chosen = above + tied * jnp.where(tie_rank < need, 1.0, 0.0)
    slot = _prefix_count(chosen) - chosen
    slot = jnp.where(chosen > 0.0, slot, -1.0).astype(jnp.int32)
    slot_ref[...] = slot
    pad = jnp.full((LANES - n_e, slot.shape[1]), -1, jnp.int32)
    for i in range(n_rows // n_e):
        slot_t_ref[i] = jnp.concatenate([slot[i * n_e:(i + 1) * n_e], pad], axis=0).T


_SELECT_BATCHES = 8


def _select(aff, cap):
    b, n_e, s = aff.shape
    nbat = _SELECT_BATCHES if b % _SELECT_BATCHES == 0 else b
    rows = nbat * n_e
    slot, slot_t = pl.pallas_call(
        functools.partial(_select_kernel, cap=cap, n_e=n_e),
        grid=(b // nbat,),
        in_specs=[pl.BlockSpec((rows, s), lambda i: (i, 0))],
        out_specs=[
            pl.BlockSpec((rows, s), lambda i: (i, 0)),
            pl.BlockSpec((nbat, s, LANES), lambda i: (i, 0, 0)),
        ],
        out_shape=[
            jax.ShapeDtypeStruct((b * n_e, s), jnp.int32),
            jax.ShapeDtypeStruct((b, s, LANES), jnp.int32),
        ],
        compiler_params=_cparams("parallel"),
        name="select",
    )(aff.reshape(b * n_e, s))
    return slot.reshape(b, n_e, s), slot_t


def _gather_kernel(slot_ref, aff_ref, h_ref, xg_ref, gate_ref):
    cap = xg_ref.shape[0]
    s = h_ref.shape[0]
    hit = slot_ref[...] == lax.broadcasted_iota(jnp.int32, (cap, s), 0)
    onehot = jnp.where(hit, 1.0, 0.0).astype(BF16)
    xg_ref[...] = _dot(onehot, h_ref[...]).astype(BF16)
    gate_ref[...] = jnp.sum(jnp.where(hit, aff_ref[...], 0.0), axis=-1, keepdims=True)


def _gather(slot, aff, h2, cap):
    b, n_e, s = slot.shape
    d = h2.shape[2]
    slot4 = slot.reshape(b, n_e, 1, s)
    aff4 = aff.reshape(b, n_e, 1, s)
    return pl.pallas_call(
        _gather_kernel,
        grid=(b, n_e),
        in_specs=[
            pl.BlockSpec((None, None, 1, s), lambda bi, e: (bi, e, 0, 0)),
            pl.BlockSpec((None, None, 1, s), lambda bi, e: (bi, e, 0, 0)),
            pl.BlockSpec((None, s, d), lambda bi, e: (bi, 0, 0)),
        ],
        out_specs=[
            pl.BlockSpec((None, None, cap, d), lambda bi, e: (bi, e, 0, 0)),
            pl.BlockSpec((None, None, cap, 1), lambda bi, e: (bi, e, 0, 0)),
        ],
        out_shape=[
            jax.ShapeDtypeStruct((b, n_e, cap, d), BF16),
            jax.ShapeDtypeStruct((b, n_e, cap, 1), F32),
        ],
        compiler_params=_cparams("parallel", "arbitrary"),
        name="gather",
    )(slot4, aff4, h2)


def _expert_kernel(xg_ref, gate_ref, wg_ref, wu_ref, wd_ref, y_ref, *, f_chunk):
    nb, cap, d = xg_ref.shape
    x = xg_ref[...].reshape(nb * cap, d)
    f = wg_ref.shape[1]
    acc = None
    for c in range(f // f_chunk):
        cs = slice(c * f_chunk, (c + 1) * f_chunk)
        a = _dot(x, wg_ref[:, cs])
        u = _dot(x, wu_ref[:, cs])
        act = (a * jax.nn.sigmoid(a) * u).astype(BF16)
        part = _dot(act, wd_ref[cs, :])
        acc = part if acc is None else acc + part
    y = acc * gate_ref[...].reshape(nb * cap, 1)
    y_ref[...] = y.reshape(nb, cap, d).astype(BF16)


def _experts(xg, gate, wg, wu, wd, layer, nb, f_chunk):
    b, n_e, cap, d = xg.shape
    f = wg.shape[3]
    return pl.pallas_call(
        functools.partial(_expert_kernel, f_chunk=f_chunk),
        grid=(n_e, b // nb),
        in_specs=[
            pl.BlockSpec((nb, None, cap, d), lambda e, bg: (bg, e, 0, 0)),
            pl.BlockSpec((nb, None, cap, 1), lambda e, bg: (bg, e, 0, 0)),
            pl.BlockSpec((None, None, d, f), lambda e, bg: (layer, e, 0, 0)),
            pl.BlockSpec((None, None, d, f), lambda e, bg: (layer, e, 0, 0)),
            pl.BlockSpec((None, None, f, d), lambda e, bg: (layer, e, 0, 0)),
        ],
        out_specs=pl.BlockSpec((nb, None, cap, d), lambda e, bg: (bg, e, 0, 0)),
        out_shape=jax.ShapeDtypeStruct((b, n_e, cap, d), BF16),
        compiler_params=_cparams("arbitrary", "arbitrary"),
        name="expert",
    )(xg, gate, wg, wu, wd)


def _combine_kernel(x_ref, gt_ref, slot_t_ref, y_ref, o_ref):
    tm = x_ref.shape[0]
    n_e, cap, d = y_ref.shape
    slot_t = slot_t_ref[...]
    lane = lax.broadcasted_iota(jnp.int32, (tm, cap), 1)
    acc = jnp.zeros((tm, d), F32)
    for e in range(n_e):
        onehot = jnp.where(slot_t[:, e:e + 1] == lane, 1.0, 0.0).astype(BF16)
        acc = acc + _dot(onehot, y_ref[e])
    o_ref[...] = x_ref[...] + gt_ref[...] * acc


def _combine(x1, gt2, slot_t, y, tm):
    b, s, d = x1.shape
    _, n_e, cap, _ = y.shape
    return pl.pallas_call(
        _combine_kernel,
        grid=(b, s // tm),
        in_specs=[
            pl.BlockSpec((None, tm, d), lambda bi, si: (bi, si, 0)),
            pl.BlockSpec((None, 1, d), lambda bi, si: (bi, 0, 0)),
            pl.BlockSpec((None, tm, LANES), lambda bi, si: (bi, si, 0)),
            pl.BlockSpec((None, n_e, cap, d), lambda bi, si: (bi, 0, 0, 0)),
        ],
        out_specs=pl.BlockSpec((None, tm, d), lambda bi, si: (bi, si, 0)),
        out_shape=jax.ShapeDtypeStruct((b, s, d), F32),
        compiler_params=_cparams("parallel", "arbitrary"),
        name="combine",
    )(x1, gt2, slot_t, y)


def kernel(x, c, w_mod, b_mod, norm1_g, norm2_g, w_in, hy_conv_w, hy_conv_b, hy_f1_w, hy_f1_b, hy_f1_freq,
           hy_f2_w, hy_f2_b, hy_f2_freq, hy_f3_w, hy_bias, q_norm_g, k_norm_g, attn_sink, gm_ln_g, gm_ln_b,
           gm_ws, gm_b, w_branch, w_out, w_router, w_e_gate, w_e_up, w_e_down):
    b, s, d = x.shape
    depth = w_mod.shape[0]
    width = hy_bias.shape[2]
    cap = EC_CAPACITY * s // N_EXPERTS
    assert w_in.shape[2] == IN_COLS and s % 512 == 0 and b % 2 == 0
    tm = 512

    cmat_np, smat_np = _dft_tables(s)
    cmat = jnp.asarray(cmat_np).astype(BF16)
    smat = jnp.asarray(smat_np).astype(BF16)
    cos_t, sin_t = _rope_tables(s)
    mod = _modulation(c, w_mod, b_mod)
    wg_all, wu_all, wd_all = (w.astype(BF16) for w in (w_e_gate, w_e_up, w_e_down))

    for l in range(depth):
        sh1, sc1, gt1, sh2, sc2, gt2 = (mod[l, j] for j in range(6))
        z_hy, z_q, z_k, z_v, z_gu, z_gv, gates = _inproj(
            x, sc1, sh1, norm1_g[l].reshape(1, d), w_in[l].astype(BF16), tm)
        pmat, qmat, nyq = _hyena_filters(s, width, hy_f1_w[l], hy_f1_b[l], hy_f1_freq[l], hy_f2_w[l],
                                         hy_f2_b[l], hy_f2_freq[l], hy_f3_w[l], cmat, smat)
        y_hy = _hyena(z_hy, hy_conv_w[l], hy_conv_b[l], pmat, qmat, nyq, hy_bias[l].T, cmat, smat)
        y_at = _attention(z_q, z_k, z_v, q_norm_g[l], k_norm_g[l], attn_sink[l], cos_t, sin_t)
        y_gm = _gmlp(z_gu, z_gv, gm_ln_g[l], gm_ln_b[l], gm_ws[l].astype(BF16), gm_b[l].T, tm)
        x1, h2, aff = _merge(x, y_hy, y_at, y_gm, gates, gt1, w_branch[l].astype(BF16),
                             w_out[l].astype(BF16), norm2_g[l].reshape(1, d), sc2, sh2, w_router[l], tm)
        slot, slot_t = _select(aff, cap)
        xg, gate = _gather(slot, aff, h2, cap)
        y = _experts(xg, gate, wg_all, wu_all, wd_all, l, nb=2, f_chunk=1024)
        x = _combine(x1, gt2, slot_t, y, tm)
    return x
```

```python
import functools
import math

import numpy as np
import jax
import jax.numpy as jnp
from jax import lax
from jax.experimental import pallas as pl
from jax.experimental.pallas import tpu as pltpu

F32 = jnp.float32
BF16 = jnp.bfloat16
HIGHEST = lax.Precision.HIGHEST

HEAD_DIM = 64
N_Q_HEADS = 8
N_KV_HEADS = 2
WINDOW = 128
ATT_BLOCK = 128
ROPE_THETA = 10000.0
HY_POS_BANDS = 8
HY_DECAY_TARGET = 1e-2
HY_FAST_DECAY = 0.3
HY_SLOW_DECAY = 1.5
HY_MOD_SHIFT = 0.05
GM_GROUPS = 4
GM_CHUNK = 128
N_EXPERTS = 16
EC_CAPACITY = 2
EPS = 1e-6
NEG_INF = -1e30

V7X_VMEM_LIMIT_BYTES = 56 * 1024 * 1024
LANES = 128


def _cparams(*sem):
    return pltpu.CompilerParams(dimension_semantics=sem, vmem_limit_bytes=V7X_VMEM_LIMIT_BYTES)


def _resident(shape, index_map):
    return pl.BlockSpec(shape, index_map, pipeline_mode=pl.Buffered(1))


def _dot(a, b, precision=None):
    return jnp.dot(a, b, preferred_element_type=F32, precision=precision)


def _dot_nt(a, b):
    return lax.dot_general(a, b, (((1,), (1,)), ((), ())), preferred_element_type=F32)


def _split_bf16(x):
    hi = x.astype(BF16)
    return hi, (x - hi.astype(F32)).astype(BF16)


def _mod_kernel(c_ref, w_ref, b_ref, o_ref):
    c = c_ref[...]
    cond = c * jax.nn.sigmoid(c)
    o_ref[...] = _dot(cond, w_ref[...], HIGHEST) + b_ref[...]


def _modulation(c, w_mod, b_mod):
    depth, d, _ = w_mod.shape
    b = c.shape[0]
    out = pl.pallas_call(
        _mod_kernel,
        grid=(depth, 6),
        in_specs=[
            pl.BlockSpec((b, d), lambda l, j: (0, 0)),
            pl.BlockSpec((None, d, d), lambda l, j: (l, 0, j)),
            pl.BlockSpec((None, None, 1, d), lambda l, j: (l, j, 0, 0)),
        ],
        out_specs=pl.BlockSpec((None, None, b, d), lambda l, j: (l, j, 0, 0)),
        out_shape=jax.ShapeDtypeStruct((depth, 6, b, d), F32),
        compiler_params=_cparams("arbitrary", "arbitrary"),
        name="mod",
    )(c, w_mod, b_mod.reshape(depth, 6, 1, d))
    return out.reshape(depth, 6, b, 1, d)


_W_HY, _W_Q, _W_KV, _W_GM, _W_GATE = 1536, 512, 128, 512, 3072
_C_Q = _W_HY
_C_K = _C_Q + _W_Q
_C_V = _C_K + _W_KV
_C_GU = _C_V + _W_KV
_C_GV = _C_GU + _W_GM
_C_GATE = _C_GV + _W_GM
IN_COLS = _C_GATE + _W_GATE


def _rms_mod(x, g, sc, sh):
    ms = jnp.mean(x * x, axis=-1, keepdims=True)
    return (x * lax.rsqrt(ms + EPS) * g) * (1.0 + sc) + sh


def _inproj_kernel(x_ref, sc_ref, sh_ref, g_ref, w_ref,
                   hy_ref, q_ref, k_ref, v_ref, gu_ref, gv_ref, gate_ref):
    hb = _rms_mod(x_ref[...], g_ref[...], sc_ref[...], sh_ref[...]).astype(BF16)

    def proj(c0, width):
        return _dot(hb, w_ref[:, c0:c0 + width])

    for j in range(3):
        hy_ref[:, j * 512:(j + 1) * 512] = proj(j * 512, 512).astype(BF16)
    q_ref[...] = proj(_C_Q, _W_Q).astype(BF16)
    k_ref[...] = proj(_C_K, _W_KV).astype(BF16)
    v_ref[...] = proj(_C_V, _W_KV).astype(BF16)
    gu_ref[...] = proj(_C_GU, _W_GM).astype(BF16)
    gv_ref[...] = proj(_C_GV, _W_GM).astype(BF16)
    for j in range(3):
        z = proj(_C_GATE + j * 1024, 1024)
        gate_ref[:, j * 1024:(j + 1) * 1024] = jax.nn.sigmoid(z).astype(BF16)


def _inproj(x, sc, sh, g, w_in_bf16, tm):
    b, s, d = x.shape
    widths = (_W_HY, _W_Q, _W_KV, _W_KV, _W_GM, _W_GM, _W_GATE)
    row = lambda bi, si: (bi, si, 0)
    vec = lambda bi, si: (bi, 0, 0)
    return pl.pallas_call(
        _inproj_kernel,
        grid=(b, s // tm),
        in_specs=[
            pl.BlockSpec((None, tm, d), row),
            pl.BlockSpec((None, 1, d), vec),
            pl.BlockSpec((None, 1, d), vec),
            pl.BlockSpec((1, d), lambda bi, si: (0, 0)),
            _resident((d, IN_COLS), lambda bi, si: (0, 0)),
        ],
        out_specs=[pl.BlockSpec((None, tm, w), row) for w in widths],
        out_shape=[jax.ShapeDtypeStruct((b, s, w), BF16) for w in widths],
        compiler_params=_cparams("parallel", "parallel"),
        name="inproj",
    )(x, sc, sh, g, w_in_bf16)


def _dft_tables(s):
    n = 2 * s
    idx = (np.arange(s, dtype=np.int64)[:, None] * np.arange(s, dtype=np.int64)[None, :]) % n
    ang = idx.astype(np.float64) * (2.0 * np.pi / n)
    return np.cos(ang).astype(np.float32), np.sin(ang).astype(np.float32)


def _filter_tables(s, width):
    f32 = np.float32
    pos = np.arange(s, dtype=f32)
    t = np.linspace(0.0, 1.0, s, dtype=f32)
    w = (f32(2.0 * math.pi) * pos / f32(s)).astype(f32)
    bands = np.linspace(1e-4, HY_POS_BANDS - 1, HY_POS_BANDS, dtype=f32)
    ang = w[:, None] * bands[None, :]
    feats = np.concatenate([t[:, None], np.cos(ang), -np.sin(ang)], axis=-1).astype(f32)
    deltas = np.abs(np.linspace(math.log(HY_DECAY_TARGET) / HY_FAST_DECAY,
                                math.log(HY_DECAY_TARGET) / HY_SLOW_DECAY, width, dtype=f32))
    window = (np.exp(-t[:, None] * deltas[None, :]) + f32(HY_MOD_SHIFT)).astype(f32)
    mirror = (s - np.arange(s)) % s
    feats_pad = np.zeros((s, LANES), f32)
    feats_pad[:, :feats.shape[1]] = feats
    window_rev = window[mirror].copy()
    window_rev[0] = 0.0
    return (np.ascontiguousarray(feats_pad.T), np.ascontiguousarray(feats_pad[mirror].T),
            np.ascontiguousarray(window.T), np.ascontiguousarray(window_rev.T))


def _alt_sign(s):
    lane = lax.broadcasted_iota(jnp.int32, (1, s), 1)
    return jnp.where(lane % 2 == 0, 1.0, -1.0).astype(F32)


def _filter_kernel(ff_ref, fr_ref, wf_ref, wr_ref, f1w_ref, f1b_ref, f1f_ref, f2w_ref, f2b_ref, f2f_ref,
                   f3f_ref, f3b_ref, c_ref, s_ref, p_ref, q_ref, kn_ref, af_ref, ar_ref):
    tr, s = p_ref.shape
    n = 2 * s

    @pl.when(pl.program_id(0) == 0)
    def _():
        def mlp(feats):
            a = jnp.sin(f1f_ref[...] * (_dot(f1w_ref[...], feats, HIGHEST) + f1b_ref[...]))
            return jnp.sin(f2f_ref[...] * (_dot(f2w_ref[...], a, HIGHEST) + f2b_ref[...]))
        af_ref[...] = mlp(ff_ref[...])
        ar_ref[...] = mlp(fr_ref[...])

    h1 = _dot(f3f_ref[...], af_ref[...], HIGHEST) * wf_ref[...]
    h2 = _dot(f3b_ref[...], ar_ref[...], HIGHEST) * wr_ref[...]
    l1 = jnp.sum(jnp.abs(h1), axis=1, keepdims=True) + jnp.sum(jnp.abs(h2), axis=1, keepdims=True)
    k1 = h1 / l1
    k2 = h2 / l1
    kk = jnp.concatenate([k1, k2], axis=0).astype(BF16)
    ck = _dot(kk, c_ref[...])
    sk = _dot(kk, s_ref[...])
    sign = _alt_sign(s)
    kc = ck[:tr] + sign * ck[tr:]
    ks = sk[:tr] + sign * sk[tr:]
    lane = lax.broadcasted_iota(jnp.int32, (1, s), 1)
    wgt = jnp.where(lane == 0, 1.0 / n, 2.0 / n).astype(F32)
    p_ref[...] = kc * wgt
    q_ref[...] = ks * wgt
    kn_ref[...] = jnp.sum(sign * (k1 + k2), axis=1, keepdims=True) * (1.0 / n)


def _hyena_filters(s, width, f1_w, f1_b, f1_freq, f2_w, f2_b, f2_freq, f3_w, cmat, smat, tr=256):
    feats_f, feats_r, win_f, win_r = (jnp.asarray(a) for a in _filter_tables(s, width))
    hid = f1_w.shape[1]
    f1w_t = jnp.zeros((hid, LANES), F32).at[:, :f1_w.shape[0]].set(f1_w.T)
    f3_t = f3_w.T
    n_row = 2 * width
    nrt = n_row // tr
    wpt = width // tr
    col = lambda v: v.reshape(hid, 1)
    full = lambda shape: pl.BlockSpec(shape, lambda j: (0,) * len(shape))
    return pl.pallas_call(
        _filter_kernel,
        grid=(nrt,),
        in_specs=[
            full((LANES, s)), full((LANES, s)),
            pl.BlockSpec((tr, s), lambda j: (j % wpt, 0)),
            pl.BlockSpec((tr, s), lambda j: (j % wpt, 0)),
            full((hid, LANES)), full((hid, 1)), full((hid, 1)),
            full((hid, hid)), full((hid, 1)), full((hid, 1)),
            pl.BlockSpec((tr, hid), lambda j: (j, 0)),
            pl.BlockSpec((tr, hid), lambda j: (j + nrt, 0)),
            _resident((s, s), lambda j: (0, 0)),
            _resident((s, s), lambda j: (0, 0)),
        ],
        out_specs=[
            pl.BlockSpec((tr, s), lambda j: (j, 0)),
            pl.BlockSpec((tr, s), lambda j: (j, 0)),
            pl.BlockSpec((tr, 1), lambda j: (j, 0)),
        ],
        out_shape=[
            jax.ShapeDtypeStruct((n_row, s), F32),
            jax.ShapeDtypeStruct((n_row, s), F32),
            jax.ShapeDtypeStruct((n_row, 1), F32),
        ],
        scratch_shapes=[pltpu.VMEM((hid, s), F32), pltpu.VMEM((hid, s), F32)],
        compiler_params=_cparams("arbitrary"),
        name="hyena_filter",
    )(feats_f, feats_r, win_f, win_r, f1w_t, col(f1_b), col(f1_freq),
      f2_w.T, col(f2_b), col(f2_freq), f3_t, f3_t, cmat, smat)


_HY_CHUNK = 512


def _hyena_kernel(zv_ref, z1_ref, z2_ref, wv_ref, w1_ref, w2_ref, bv_ref, b1_ref, b2_ref,
                  p0_ref, q0_ref, n0_ref, p1_ref, q1_ref, n1_ref, bias_ref, c_ref, s_ref, o_ref,
                  u_ref, x1_ref, x2_ref, yr_ref, yi_ref):
    nb, s, tc = o_ref.shape
    row = lax.broadcasted_iota(jnp.int32, (s, 1), 0)
    sign = _alt_sign(s)
    chunks = [slice(c0, c0 + _HY_CHUNK) for c0 in range(0, s, _HY_CHUNK)]

    def short_conv_t(z_ref, w_ref, b_ref, dst_ref):
        for i in range(nb):
            z = z_ref[i].astype(F32)
            zp = jnp.where(row == 0, 0.0, pltpu.roll(z, 1, 0))
            zn = jnp.where(row == s - 1, 0.0, pltpu.roll(z, s - 1, 0))
            y = zp * w_ref[0:1, :] + z * w_ref[1:2, :] + zn * w_ref[2:3, :] + b_ref[...]
            dst_ref[i * tc:(i + 1) * tc, :] = y.T.astype(BF16)

    def per_batch(t):
        return jnp.concatenate([t] * nb, axis=0)

    def long_conv(p_ref, q_ref, n_ref, bias, x_ref, emit):
        nyq = jnp.sum(sign * u_ref[...].astype(F32), axis=1, keepdims=True) * per_batch(n_ref[...])
        for ch in chunks:
            a = _dot(u_ref[...], c_ref[:, ch])
            b = _dot(u_ref[...], s_ref[:, ch])
            p = per_batch(p_ref[:, ch])
            q = per_batch(q_ref[:, ch])
            yr_ref[:, ch] = (a * p - b * q).astype(BF16)
            yi_ref[:, ch] = (a * q + b * p).astype(BF16)
        for ch in chunks:
            y = _dot(yr_ref[...], c_ref[:, ch]) + _dot(yi_ref[...], s_ref[:, ch])
            y = y + sign[:, ch] * nyq + u_ref[:, ch].astype(F32) * bias
            emit(ch, x_ref[:, ch].astype(F32) * y)

    short_conv_t(zv_ref, wv_ref, bv_ref, u_ref)
    short_conv_t(z1_ref, w1_ref, b1_ref, x1_ref)
    short_conv_t(z2_ref, w2_ref, b2_ref, x2_ref)

    def to_u(ch, y):
        u_ref[:, ch] = y.astype(BF16)

    def to_out(ch, y):
        for i in range(nb):
            o_ref[i, ch, :] = y[i * tc:(i + 1) * tc].T.astype(BF16)

    long_conv(p0_ref, q0_ref, n0_ref, per_batch(bias_ref[:, 0:1]), x1_ref, to_u)
    long_conv(p1_ref, q1_ref, n1_ref, per_batch(bias_ref[:, 1:2]), x2_ref, to_out)


def _hyena(z_hy, conv_w, conv_b, pmat, qmat, nyq, bias_t, cmat, smat, tc=256, nb=2):
    b, s, _ = z_hy.shape
    width = bias_t.shape[0]
    nct = width // tc
    conv_b = conv_b.reshape(1, 3 * width)
    zcol = lambda k: pl.BlockSpec((nb, s, tc), lambda ci, bi, k=k: (bi, 0, ci + k * nct),
                                  pipeline_mode=pl.Buffered(1))
    wcol = lambda rows, k: pl.BlockSpec((rows, tc), lambda ci, bi, k=k: (0, ci + k * nct))
    spec = lambda k: _resident((tc, s), lambda ci, bi, k=k: (ci + k * nct, 0))
    nspec = lambda k: pl.BlockSpec((tc, 1), lambda ci, bi, k=k: (ci + k * nct, 0))
    m = nb * tc
    return pl.pallas_call(
        _hyena_kernel,
        grid=(nct, b // nb),
        in_specs=[
            zcol(0), zcol(1), zcol(2),
            wcol(3, 0), wcol(3, 1), wcol(3, 2),
            wcol(1, 0), wcol(1, 1), wcol(1, 2),
            spec(0), spec(0), nspec(0),
            spec(1), spec(1), nspec(1),
            pl.BlockSpec((tc, 2), lambda ci, bi: (ci, 0)),
            _resident((s, s), lambda ci, bi: (0, 0)),
            _resident((s, s), lambda ci, bi: (0, 0)),
        ],
        out_specs=pl.BlockSpec((nb, s, tc), lambda ci, bi: (bi, 0, ci)),
        out_shape=jax.ShapeDtypeStruct((b, s, width), BF16),
        scratch_shapes=[pltpu.VMEM((m, s), BF16) for _ in range(5)],
        compiler_params=_cparams("arbitrary", "arbitrary"),
        name="hyena",
    )(z_hy, z_hy, z_hy, conv_w, conv_w, conv_w, conv_b, conv_b, conv_b,
      pmat, qmat, nyq, pmat, qmat, nyq, bias_t, cmat, smat)


def _rope_tables(s):
    pos = jnp.arange(s, dtype=F32)
    inv = ROPE_THETA ** (-jnp.arange(0, HEAD_DIM, 2, dtype=F32) / HEAD_DIM)
    ang = pos[:, None] * inv[None, :]
    cos, sin = jnp.cos(ang), jnp.sin(ang)
    reps = LANES // HEAD_DIM
    return (jnp.tile(jnp.concatenate([cos, cos], axis=1), (1, reps)),
            jnp.tile(jnp.concatenate([-sin, sin], axis=1), (1, reps)))


def _head_mean_matrix(width):
    i = np.arange(width)
    return jnp.asarray((i[:, None] // HEAD_DIM == i[None, :] // HEAD_DIM).astype(np.float32) / HEAD_DIM)


def _norm_rope(z, g, hm, cos, sin):
    width = z.shape[1]
    hi, lo = _split_bf16(z * z)
    ms = _dot(hi, hm) + _dot(lo, hm)
    zn = z * lax.rsqrt(ms + EPS) * g
    lane = lax.broadcasted_iota(jnp.int32, (1, width), 1)
    half = HEAD_DIM // 2
    partner = jnp.where(lane % HEAD_DIM < half, pltpu.roll(zn, width - half, 1), pltpu.roll(zn, half, 1))
    return zn * cos + partner * sin


_ATT_PREP_ROWS = 256


def _attn_kernel(sink_ref, q_ref, k_ref, v_ref, qg_ref, kg_ref, hmq_ref, hmk_ref, cos_ref, sin_ref,
                 o_ref, qr_ref, kp_ref, vp_ref):
    s, wq = q_ref.shape
    blk = ATT_BLOCK
    span = 3 * blk
    reps = wq // LANES
    lane = lax.broadcasted_iota(jnp.int32, (1, LANES), 1)
    low = lane < HEAD_DIM

    def prep(c, carry):
        rows = pl.ds(pl.multiple_of(c * _ATT_PREP_ROWS, _ATT_PREP_ROWS), _ATT_PREP_ROWS)
        cos = cos_ref[rows, :]
        sin = sin_ref[rows, :]
        q = _norm_rope(q_ref[rows, :].astype(F32), qg_ref[...], hmq_ref[...],
                       jnp.concatenate([cos] * reps, axis=1), jnp.concatenate([sin] * reps, axis=1))
        qr_ref[rows, :] = (q * (HEAD_DIM ** -0.5)).astype(BF16)
        k = _norm_rope(k_ref[rows, :].astype(F32), kg_ref[...], hmk_ref[...], cos, sin)
        v = v_ref[rows, :].astype(F32)
        for src, dst in ((k, kp_ref), (v, vp_ref)):
            swapped = pltpu.roll(src, HEAD_DIM, 1)
            dst[0, 0, rows, 0:LANES] = jnp.where(low, src, 0.0).astype(BF16)
            dst[0, 1, rows, 0:LANES] = jnp.where(low, 0.0, swapped).astype(BF16)
            dst[1, 0, rows, 0:LANES] = jnp.where(low, swapped, 0.0).astype(BF16)
            dst[1, 1, rows, 0:LANES] = jnp.where(low, 0.0, src).astype(BF16)
        ones_lo = jnp.broadcast_to(jnp.where(low, 1.0, 0.0), (_ATT_PREP_ROWS, LANES)).astype(BF16)
        ones_hi = jnp.broadcast_to(jnp.where(low, 0.0, 1.0), (_ATT_PREP_ROWS, LANES)).astype(BF16)
        for kh in range(N_KV_HEADS):
            vp_ref[kh, 0, rows, LANES:2 * LANES] = ones_lo
            vp_ref[kh, 1, rows, LANES:2 * LANES] = ones_hi
        return carry

    lax.fori_loop(0, s // _ATT_PREP_ROWS, prep, 0)

    def block(i, carry):
        q0 = pl.multiple_of(i * blk, blk)
        k0 = pl.multiple_of(jnp.clip(q0 - blk, 0, s - span), blk)
        qrows = pl.ds(q0, blk)
        krows = pl.ds(k0, span)
        rel = (lax.broadcasted_iota(jnp.int32, (blk, span), 1) + (k0 - q0)
               - lax.broadcasted_iota(jnp.int32, (blk, span), 0))
        valid = jnp.abs(rel) <= WINDOW
        valid = jnp.concatenate([valid, valid], axis=0)
        for kh in range(N_KV_HEADS):
            c0 = kh * 2 * LANES
            qs = jnp.concatenate([qr_ref[qrows, c0:c0 + LANES], qr_ref[qrows, c0 + LANES:c0 + 2 * LANES]],
                                 axis=0)
            es, sinks = [], []
            for r in range(2):
                sc = _dot_nt(qs, kp_ref[kh, r, krows, :])
                sc = jnp.where(valid, sc, NEG_INF)
                sink = jnp.concatenate([jnp.full((blk, LANES), sink_ref[4 * kh + r], F32),
                                        jnp.full((blk, LANES), sink_ref[4 * kh + 2 + r], F32)], axis=0)
                m = jnp.maximum(jnp.max(sc, axis=-1, keepdims=True), sink)
                es.append(jnp.exp(sc - jnp.concatenate([m] * (span // LANES), axis=1)).astype(BF16))
                sinks.append(jnp.exp(sink - m))
            od = _dot(es[0], vp_ref[kh, 0, krows, :]) + _dot(es[1], vp_ref[kh, 1, krows, :])
            o = od[:, :LANES] / (od[:, LANES:] + jnp.where(low, sinks[0], sinks[1]))
            o_ref[qrows, c0:c0 + LANES] = o[:blk].astype(BF16)
            o_ref[qrows, c0 + LANES:c0 + 2 * LANES] = o[blk:].astype(BF16)
        return carry

    lax.fori_loop(0, s // blk, block, 0, unroll=2)


def _attention(zq, zk, zv, q_g, k_g, sink, cos_t, sin_t):
    b, s, wq = zq.shape
    wk = zk.shape[2]
    assert wq == N_Q_HEADS * HEAD_DIM and wk == N_KV_HEADS * HEAD_DIM == LANES
    full = lambda shape: pl.BlockSpec(shape, lambda bi: (0,) * len(shape))
    per_b = lambda w: pl.BlockSpec((None, s, w), lambda bi: (bi, 0, 0))
    return pl.pallas_call(
        _attn_kernel,
        grid=(b,),
        in_specs=[
            pl.BlockSpec(memory_space=pltpu.SMEM),
            per_b(wq), per_b(wk), per_b(wk),
            full((1, wq)), full((1, wk)), full((wq, wq)), full((wk, wk)),
            full((s, LANES)), full((s, LANES)),
        ],
        out_specs=per_b(wq),
        out_shape=jax.ShapeDtypeStruct((b, s, wq), BF16),
        scratch_shapes=[pltpu.VMEM((s, wq), BF16),
                        pltpu.VMEM((N_KV_HEADS, 2, s, LANES), BF16),
                        pltpu.VMEM((N_KV_HEADS, 2, s, 2 * LANES), BF16)],
        compiler_params=_cparams("parallel"),
        name="attn",
    )(sink, zq, zk, zv, jnp.tile(q_g, wq // HEAD_DIM).reshape(1, wq),
      jnp.tile(k_g, wk // HEAD_DIM).reshape(1, wk),
      _head_mean_matrix(wq).astype(BF16), _head_mean_matrix(wk).astype(BF16), cos_t, sin_t)


def _gelu(x):
    return 0.5 * x * (1.0 + lax.erf(x * (1.0 / math.sqrt(2.0))))


def _gmlp_kernel(zu_ref, zv_ref, g_ref, b_ref, ws_ref, pb_ref, o_ref):
    tm, width = o_ref.shape
    gch = width // GM_GROUPS
    u = _gelu(zu_ref[...].astype(F32))
    v = _gelu(zv_ref[...].astype(F32))
    mu = jnp.mean(v, axis=-1, keepdims=True)
    var = jnp.mean(jnp.square(v - mu), axis=-1, keepdims=True)
    v = ((v - mu) * lax.rsqrt(var + EPS) * g_ref[...] + b_ref[...]).astype(BF16)
    for c in range(tm // GM_CHUNK):
        r0 = c * GM_CHUNK
        cols = []
        for g in range(GM_GROUPS):
            sv = _dot(ws_ref[g], v[r0:r0 + GM_CHUNK, g * gch:(g + 1) * gch]) + pb_ref[:, g:g + 1]
            cols.append(sv)
        sv = jnp.concatenate(cols, axis=1)
        o_ref[r0:r0 + GM_CHUNK, :] = (u[r0:r0 + GM_CHUNK, :] * sv).astype(BF16)


def _gmlp(zu, zv, ln_g, ln_b, ws_bf16, pos_bias_t, tm):
    b, s, width = zu.shape
    row = lambda bi, si: (bi, si, 0)
    full = lambda shape: pl.BlockSpec(shape, lambda bi, si: (0,) * len(shape))
    return pl.pallas_call(
        _gmlp_kernel,
        grid=(b, s // tm),
        in_specs=[
            pl.BlockSpec((None, tm, width), row), pl.BlockSpec((None, tm, width), row),
            full((1, width)), full((1, width)),
            full((GM_GROUPS, GM_CHUNK, GM_CHUNK)), full((GM_CHUNK, GM_GROUPS)),
        ],
        out_specs=pl.BlockSpec((None, tm, width), row),
        out_shape=jax.ShapeDtypeStruct((b, s, width), BF16),
        compiler_params=_cparams("parallel", "parallel"),
        name="gmlp",
    )(zu, zv, ln_g.reshape(1, width), ln_b.reshape(1, width), ws_bf16, pos_bias_t)


_MERGE_ROWS = 256


def _merge_kernel(x_ref, yh_ref, ya_ref, yg_ref, gate_ref, gt_ref, wb_ref, wo_ref,
                  g2_ref, sc2_ref, sh2_ref, wrh_ref, wrl_ref, x1_ref, h2_ref, aff_ref):
    tm, d = x_ref.shape
    n_e = aff_ref.shape[0]
    for r0 in range(0, tm, _MERGE_ROWS):
        rows = slice(r0, r0 + _MERGE_ROWS)
        acc = None
        for j, y_ref in enumerate((yh_ref, ya_ref, yg_ref)):
            br = _dot(y_ref[rows, :], wb_ref[j])
            term = gate_ref[rows, j * d:(j + 1) * d].astype(F32) * br
            acc = term if acc is None else acc + term
        mix = _dot(acc.astype(BF16), wo_ref[...])
        x1 = x_ref[rows, :] + gt_ref[...] * mix
        x1_ref[rows, :] = x1
        h2 = _rms_mod(x1, g2_ref[...], sc2_ref[...], sh2_ref[...])
        h2_ref[rows, :] = h2.astype(BF16)
        hi, lo = _split_bf16(h2)
        logits = _dot(hi, wrh_ref[...]) + _dot(lo, wrh_ref[...]) + _dot(hi, wrl_ref[...])
        logits = logits.T[:n_e]
        m = jnp.max(logits, axis=0, keepdims=True)
        e = jnp.exp(logits - m)
        aff_ref[:, rows] = e / jnp.sum(e, axis=0, keepdims=True)


def _merge(x, y_hy, y_at, y_gm, gates, gt1, wb_bf16, wo_bf16, g2, sc2, sh2, w_router, tm):
    b, s, d = x.shape
    width = y_hy.shape[2]
    n_e = w_router.shape[1]
    wr = jnp.zeros((d, LANES), F32).at[:, :n_e].set(w_router)
    wr_hi = wr.astype(BF16)
    wr_lo = (wr - wr_hi.astype(F32)).astype(BF16)
    row = lambda bi, si: (bi, si, 0)
    vec = lambda bi, si: (bi, 0, 0)
    full = lambda shape: pl.BlockSpec(shape, lambda bi, si: (0,) * len(shape))
    return pl.pallas_call(
        _merge_kernel,
        grid=(b, s // tm),
        in_specs=[
            pl.BlockSpec((None, tm, d), row),
            pl.BlockSpec((None, tm, width), row), pl.BlockSpec((None, tm, width), row),
            pl.BlockSpec((None, tm, width), row),
            pl.BlockSpec((None, tm, 3 * d), row),
            pl.BlockSpec((None, 1, d), vec),
            _resident((3, width, d), lambda bi, si: (0, 0, 0)),
            _resident((d, d), lambda bi, si: (0, 0)),
            full((1, d)),
            pl.BlockSpec((None, 1, d), vec), pl.BlockSpec((None, 1, d), vec),
            full((d, LANES)), full((d, LANES)),
        ],
        out_specs=[
            pl.BlockSpec((None, tm, d), row),
            pl.BlockSpec((None, tm, d), row),
            pl.BlockSpec((None, n_e, tm), lambda bi, si: (bi, 0, si)),
        ],
        out_shape=[
            jax.ShapeDtypeStruct((b, s, d), F32),
            jax.ShapeDtypeStruct((b, s, d), BF16),
            jax.ShapeDtypeStruct((b, n_e, s), F32),
        ],
        compiler_params=_cparams("parallel", "parallel"),
        name="merge",
    )(x, y_hy, y_at, y_gm, gates, gt1, wb_bf16, wo_bf16, g2, sc2, sh2, wr_hi, wr_lo)


def _prefix_count(x):
    n = x.shape[-1]
    lane = lax.broadcasted_iota(jnp.int32, x.shape, x.ndim - 1)
    shift = 1
    while shift < n:
        x = x + jnp.where(lane >= shift, pltpu.roll(x, shift, x.ndim - 1), 0.0)
        shift *= 2
    return x


_REFINE_STEPS = 30


def _select_kernel(aff_ref, slot_ref, slot_t_ref, *, cap, n_e):
    aff = aff_ref[...]
    n_rows = aff.shape[0]

    def count_ge(t):
        return jnp.sum(jnp.where(aff >= t, 1.0, 0.0), axis=-1, keepdims=True)

    def coarse(j, cur):
        cand = cur | (jnp.int32(1) << (30 - j))
        return jnp.where(count_ge(pltpu.bitcast(cand, F32)) >= cap, cand, cur)

    bits = lax.fori_loop(0, 31, coarse, jnp.zeros((n_rows, 1), jnp.int32))
    lo = pltpu.bitcast(bits, F32)
    hi = pltpu.bitcast(bits + 1, F32)

    def refine(j, lohi):
        lo, hi = lohi
        mid = lo + 0.5 * (hi - lo)
        ok = count_ge(mid) >= cap
        return jnp.where(ok, mid, lo), jnp.where(ok, hi, mid)

    lo, hi = lax.fori_loop(0, _REFINE_STEPS, refine, (lo, hi))
    thr = jnp.min(jnp.where(aff >= lo, aff, jnp.inf), axis=-1, keepdims=True)

    above = jnp.where(aff > thr, 1.0, 0.0)
    tied = jnp.where(aff == thr, 1.0, 0.0)
    need = cap - jnp.sum(above, axis=-1, keepdims=True)
    tie_rank = _prefix_count(tied) - tied
    chosen = above + tied * jnp.where(tie_rank < need, 1.0, 0.0)
    slot = _prefix_count(chosen) - chosen
    slot = jnp.where(chosen > 0.0, slot, -1.0).astype(jnp.int32)
    slot_ref[...] = slot
    pad = jnp.full((LANES - n_e, slot.shape[1]), -1, jnp.int32)
    for i in range(n_rows // n_e):
        slot_t_ref[i] = jnp.concatenate([slot[i * n_e:(i + 1) * n_e], pad], axis=0).T


_SELECT_BATCHES = 8


def _select(aff, cap):
    b, n_e, s = aff.shape
    nbat = _SELECT_BATCHES if b % _SELECT_BATCHES == 0 else b
    rows = nbat * n_e
    slot, slot_t = pl.pallas_call(
        functools.partial(_select_kernel, cap=cap, n_e=n_e),
        grid=(b // nbat,),
        in_specs=[pl.BlockSpec((rows, s), lambda i: (i, 0))],
        out_specs=[
            pl.BlockSpec((rows, s), lambda i: (i, 0)),
            pl.BlockSpec((nbat, s, LANES), lambda i: (i, 0, 0)),
        ],
        out_shape=[
            jax.ShapeDtypeStruct((b * n_e, s), jnp.int32),
            jax.ShapeDtypeStruct((b, s, LANES), jnp.int32),
        ],
        compiler_params=_cparams("parallel"),
        name="select",
    )(aff.reshape(b * n_e, s))
    return slot.reshape(b, n_e, s), slot_t


def _gather_kernel(slot_ref, aff_ref, h_ref, xg_ref, gate_ref):
    cap = xg_ref.shape[0]
    s = h_ref.shape[0]
    hit = slot_ref[...] == lax.broadcasted_iota(jnp.int32, (cap, s), 0)
    onehot = jnp.where(hit, 1.0, 0.0).astype(BF16)
    xg_ref[...] = _dot(onehot, h_ref[...]).astype(BF16)
    gate_ref[...] = jnp.sum(jnp.where(hit, aff_ref[...], 0.0), axis=-1, keepdims=True)


def _gather(slot, aff, h2, cap):
    b, n_e, s = slot.shape
    d = h2.shape[2]
    slot4 = slot.reshape(b, n_e, 1, s)
    aff4 = aff.reshape(b, n_e, 1, s)
    return pl.pallas_call(
        _gather_kernel,
        grid=(b, n_e),
        in_specs=[
            pl.BlockSpec((None, None, 1, s), lambda bi, e: (bi, e, 0, 0)),
            pl.BlockSpec((None, None, 1, s), lambda bi, e: (bi, e, 0, 0)),
            pl.BlockSpec((None, s, d), lambda bi, e: (bi, 0, 0)),
        ],
        out_specs=[
            pl.BlockSpec((None, None, cap, d), lambda bi, e: (bi, e, 0, 0)),
            pl.BlockSpec((None, None, cap, 1), lambda bi, e: (bi, e, 0, 0)),
        ],
        out_shape=[
            jax.ShapeDtypeStruct((b, n_e, cap, d), BF16),
            jax.ShapeDtypeStruct((b, n_e, cap, 1), F32),
        ],
        compiler_params=_cparams("parallel", "arbitrary"),
        name="gather",
    )(slot4, aff4, h2)


def _expert_kernel(xg_ref, gate_ref, wg32_ref, wu32_ref, wd32_ref, y_ref, wg_ref, wu_ref, wd_ref, *, f_chunk):
    nb, cap, d = xg_ref.shape
    f = wg_ref.shape[1]

    @pl.when(pl.program_id(1) == 0)
    def _():
        for src, dst in ((wg32_ref, wg_ref), (wu32_ref, wu_ref), (wd32_ref, wd_ref)):
            rows = src.shape[0]
            for r0 in range(0, rows, 256):
                dst[r0:r0 + 256, :] = src[r0:r0 + 256, :].astype(BF16)

    x = xg_ref[...].reshape(nb * cap, d)
    acc = None
    for c in range(f // f_chunk):
        cs = slice(c * f_chunk, (c + 1) * f_chunk)
        a = _dot(x, wg_ref[:, cs])
        u = _dot(x, wu_ref[:, cs])
        act = (a * jax.nn.sigmoid(a) * u).astype(BF16)
        part = _dot(act, wd_ref[cs, :])
        acc = part if acc is None else acc + part
    y = acc * gate_ref[...].reshape(nb * cap, 1)
    y_ref[...] = y.reshape(nb, cap, d).astype(BF16)


def _experts(xg, gate, wg, wu, wd, layer, nb, f_chunk):
    b, n_e, cap, d = xg.shape
    f = wg.shape[3]
    return pl.pallas_call(
        functools.partial(_expert_kernel, f_chunk=f_chunk),
        grid=(n_e, b // nb),
        in_specs=[
            pl.BlockSpec((nb, None, cap, d), lambda e, bg: (bg, e, 0, 0)),
            pl.BlockSpec((nb, None, cap, 1), lambda e, bg: (bg, e, 0, 0)),
            _resident((None, None, d, f), lambda e, bg: (layer, e, 0, 0)),
            _resident((None, None, d, f), lambda e, bg: (layer, e, 0, 0)),
            _resident((None, None, f, d), lambda e, bg: (layer, e, 0, 0)),
        ],
        scratch_shapes=[pltpu.VMEM((d, f), BF16), pltpu.VMEM((d, f), BF16), pltpu.VMEM((f, d), BF16)],
        out_specs=pl.BlockSpec((nb, None, cap, d), lambda e, bg: (bg, e, 0, 0)),
        out_shape=jax.ShapeDtypeStruct((b, n_e, cap, d), BF16),
        compiler_params=_cparams("arbitrary", "arbitrary"),
        name="expert",
    )(xg, gate, wg, wu, wd)


def _combine_kernel(x_ref, gt_ref, slot_t_ref, y_ref, o_ref):
    tm = x_ref.shape[0]
    n_e, cap, d = y_ref.shape
    slot_t = slot_t_ref[...]
    lane = lax.broadcasted_iota(jnp.int32, (tm, cap), 1)
    acc = jnp.zeros((tm, d), F32)
    for e in range(n_e):
        onehot = jnp.where(slot_t[:, e:e + 1] == lane, 1.0, 0.0).astype(BF16)
        acc = acc + _dot(onehot, y_ref[e])
    o_ref[...] = x_ref[...] + gt_ref[...] * acc


def _combine(x1, gt2, slot_t, y, tm):
    b, s, d = x1.shape
    _, n_e, cap, _ = y.shape
    return pl.pallas_call(
        _combine_kernel,
        grid=(b, s // tm),
        in_specs=[
            pl.BlockSpec((None, tm, d), lambda bi, si: (bi, si, 0)),
            pl.BlockSpec((None, 1, d), lambda bi, si: (bi, 0, 0)),
            pl.BlockSpec((None, tm, LANES), lambda bi, si: (bi, si, 0)),
            pl.BlockSpec((None, n_e, cap, d), lambda bi, si: (bi, 0, 0, 0)),
        ],
        out_specs=pl.BlockSpec((None, tm, d), lambda bi, si: (bi, si, 0)),
        out_shape=jax.ShapeDtypeStruct((b, s, d), F32),
        compiler_params=_cparams("parallel", "arbitrary"),
        name="combine",
    )(x1, gt2, slot_t, y)


def kernel(x, c, w_mod, b_mod, norm1_g, norm2_g, w_in, hy_conv_w, hy_conv_b, hy_f1_w, hy_f1_b, hy_f1_freq,
           hy_f2_w, hy_f2_b, hy_f2_freq, hy_f3_w, hy_bias, q_norm_g, k_norm_g, attn_sink, gm_ln_g, gm_ln_b,
           gm_ws, gm_b, w_branch, w_out, w_router, w_e_gate, w_e_up, w_e_down):
    b, s, d = x.shape
    depth = w_mod.shape[0]
    width = hy_bias.shape[2]
    cap = EC_CAPACITY * s // N_EXPERTS
    assert w_in.shape[2] == IN_COLS and s % 512 == 0 and b % 2 == 0
    tm = 512

    cmat_np, smat_np = _dft_tables(s)
    cmat = jnp.asarray(cmat_np).astype(BF16)
    smat = jnp.asarray(smat_np).astype(BF16)
    cos_t, sin_t = _rope_tables(s)
    mod = _modulation(c, w_mod, b_mod)

    for l in range(depth):
        sh1, sc1, gt1, sh2, sc2, gt2 = (mod[l, j] for j in range(6))
        z_hy, z_q, z_k, z_v, z_gu, z_gv, gates = _inproj(
            x, sc1, sh1, norm1_g[l].reshape(1, d), w_in[l].astype(BF16), tm)
        pmat, qmat, nyq = _hyena_filters(s, width, hy_f1_w[l], hy_f1_b[l], hy_f1_freq[l], hy_f2_w[l],
                                         hy_f2_b[l], hy_f2_freq[l], hy_f3_w[l], cmat, smat)
        y_hy = _hyena(z_hy, hy_conv_w[l], hy_conv_b[l], pmat, qmat, nyq, hy_bias[l].T, cmat, smat)
        y_at = _attention(z_q, z_k, z_v, q_norm_g[l], k_norm_g[l], attn_sink[l], cos_t, sin_t)
        y_gm = _gmlp(z_gu, z_gv, gm_ln_g[l], gm_ln_b[l], gm_ws[l].astype(BF16), gm_b[l].T, tm)
        x1, h2, aff = _merge(x, y_hy, y_at, y_gm, gates, gt1, w_branch[l].astype(BF16),
                             w_out[l].astype(BF16), norm2_g[l].reshape(1, d), sc2, sh2, w_router[l], tm)
        slot, slot_t = _select(aff, cap)
        xg, gate = _gather(slot, aff, h2, cap)
        y = _experts(xg, gate, w_e_gate, w_e_up, w_e_down, l, nb=2, f_chunk=1024)
        x = _combine(x1, gt2, slot_t, y, tm)
    return x
```
